```python
import math
import jax, jax.numpy as jnp
from jax import lax
import numpy as np

D_MODEL = 1024
BATCH = 8
SEQ = 4096
DEPTH = 2
DEC_BATCH = 32
DEC_SEQ = 1
PAST_LEN = 16384
PAGE_SIZE = 128

SB_HEADS = 8
SB_HEAD_DIM = D_MODEL // 16
SB_WIDTH = SB_HEADS * SB_HEAD_DIM
SSM_HEAD_DIM = 64
SSM_WIDTH = D_MODEL // 2
SSM_HEADS = SSM_WIDTH // SSM_HEAD_DIM
SSM_GROUPS = 2
SSM_STATE = 64
SSM_CONV = 4
SSM_CONV_DIM = SSM_WIDTH + 2 * SSM_GROUPS * SSM_STATE
SSM_CHUNK = 128
DIFF_HEAD_DIM = 64
DIFF_HEADS = D_MODEL // 256
DIFF_WIDTH = DIFF_HEADS * 2 * DIFF_HEAD_DIM
ROPE_THETA = 10000.0
MOE_GROUPS = 4
MOE_EXPERTS_PER_GROUP = 4
MOE_EXPERTS = MOE_GROUPS * MOE_EXPERTS_PER_GROUP
MOE_TOP_K = 2
MOE_FF = D_MODEL // 2
Q_BLOCK = 128
EPS = 1e-6
IN_DIM = 3 * SB_WIDTH + (SSM_WIDTH + SSM_CONV_DIM + SSM_HEADS) + 3 * DIFF_WIDTH + 3 * D_MODEL

kernel_name = 'hybrid_sb_ssd_diffattn_hmoe_step'


def _split_points():
    widths = [SB_WIDTH] * 3 + [SSM_WIDTH, SSM_CONV_DIM, SSM_HEADS] + [DIFF_WIDTH] * 3 + [D_MODEL] * 3
    pts, acc = [], 0
    for w in widths[:-1]:
        acc += w
        pts.append(acc)
    return pts


def rms_norm(x, w):
    xf = x.astype(jnp.float32)
    y = xf * lax.rsqrt(jnp.mean(xf * xf, axis=-1, keepdims=True) + EPS)
    return (y * w.astype(jnp.float32)).astype(x.dtype)


def rope(x, pos):
    half = x.shape[-1] // 2
    inv_freq = ROPE_THETA ** (-jnp.arange(half, dtype=jnp.float32) / half)
    ang = pos.astype(jnp.float32)[:, None] * inv_freq[None, :]
    shape = (1, pos.shape[0]) + (1,) * (x.ndim - 3) + (half,)
    cos, sin = jnp.cos(ang).reshape(shape), jnp.sin(ang).reshape(shape)
    xf = x.astype(jnp.float32)
    x1, x2 = xf[..., :half], xf[..., half:]
    return jnp.concatenate([x1 * cos - x2 * sin, x2 * cos + x1 * sin], axis=-1).astype(x.dtype)


def sweep_query_blocks(fn, q, q_pos):
    b, t = q.shape[:2]
    nb = t // Q_BLOCK
    q_blocks = jnp.swapaxes(q.reshape((b, nb, Q_BLOCK) + q.shape[2:]), 0, 1)
    out = lax.map(lambda a: fn(a[0], a[1]), (q_blocks, q_pos.reshape(nb, Q_BLOCK)))
    return jnp.swapaxes(out, 0, 1).reshape((b, t) + out.shape[3:])


def sb_block(q, k, v, q_pos, k_pos):
    z = jnp.einsum('bqhd,bshd->bhqs', q, k).astype(jnp.float32) / math.sqrt(SB_HEAD_DIM)
    mask = k_pos[None, :] < q_pos[:, None]
    log_1m = jnp.where(mask, jax.nn.log_sigmoid(-z), 0.0)
    after = lax.cumsum(log_1m, axis=3, reverse=True) - log_1m
    w = jnp.where(mask, jnp.exp(jax.nn.log_sigmoid(z) + after), 0.0)
    return jnp.einsum('bhqs,bshd->bqhd', w.astype(v.dtype), v)


def diff_block(q, k, v, lam, q_pos, k_pos):
    mask = k_pos[None, :] <= q_pos[:, None]
    s = jnp.einsum('bqhcd,bshcd->bchqs', q, k).astype(jnp.float32) / math.sqrt(DIFF_HEAD_DIM)
    p = jax.nn.softmax(jnp.where(mask, s, -jnp.inf), axis=-1)
    a = p[:, 0] - lam * p[:, 1]
    return jnp.einsum('bhqs,bshe->bqhe', a.astype(v.dtype), v)


def ssd(x, dt, a, bmat, cmat, h0, chunk):
    b, t, h, p = x.shape
    nc = t // chunk
    f32 = jnp.float32
    r = lambda u: u.astype(f32).reshape((b, nc, chunk) + u.shape[2:])
    xc, dtc, bc, cc = r(x), r(dt), r(bmat), r(cmat)
    cum = jnp.cumsum(dtc * a, axis=2)
    diff = cum[:, :, :, None, :] - cum[:, :, None, :, :]
    causal = jnp.tril(jnp.ones((chunk, chunk), dtype=bool))[None, None, :, :, None]
    decay = jnp.exp(jnp.where(causal, diff, -jnp.inf))
    cb = jnp.einsum('bcihn,bcjhn->bcijh', cc, bc)
    y_intra = jnp.einsum('bcijh,bcjh,bcjhp->bcihp', cb * decay, dtc, xc)
    w_end = jnp.exp(cum[:, :, -1:, :] - cum) * dtc
    s_chunk = jnp.einsum('bcjh,bcjhn,bcjhp->bchpn', w_end, bc, xc)
    chunk_decay = jnp.exp(cum[:, :, -1, :])

    def step(h_prev, inp):
        s_c, d_c = inp
        return d_c[:, :, None, None] * h_prev + s_c, h_prev

    h_last, h_in = lax.scan(step, h0.astype(f32), (jnp.swapaxes(s_chunk, 0, 1), jnp.swapaxes(chunk_decay, 0, 1)))
    h_in = jnp.swapaxes(h_in, 0, 1)
    y_inter = jnp.einsum('bcihn,bchpn,bcih->bcihp', cc, h_in, jnp.exp(cum))
    return (y_intra + y_inter).reshape(b, t, h, p), h_last


def mamba_branch(z, xbc, dt_raw, conv_prev, ssm_prev, conv_w, conv_b, dt_bias, a_log, d_skip, norm_w):
    b, t, _ = xbc.shape
    xpad = jnp.concatenate([conv_prev.astype(xbc.dtype), xbc], axis=1)
    conv = conv_b
    for tap in range(SSM_CONV):
        conv = conv + xpad[:, tap:tap + t, :] * conv_w[tap]
    conv = jax.nn.silu(conv)
    new_conv = xpad[:, t:, :]
    gn = SSM_GROUPS * SSM_STATE
    rep = SSM_HEADS // SSM_GROUPS
    xs = conv[..., :SSM_WIDTH].reshape(b, t, SSM_HEADS, SSM_HEAD_DIM)
    bm = jnp.repeat(conv[..., SSM_WIDTH:SSM_WIDTH + gn].reshape(b, t, SSM_GROUPS, SSM_STATE), rep, axis=2)
    cm = jnp.repeat(conv[..., SSM_WIDTH + gn:].reshape(b, t, SSM_GROUPS, SSM_STATE), rep, axis=2)
    dt = jax.nn.softplus(dt_raw.astype(jnp.float32) + dt_bias.astype(jnp.float32))
    a = -jnp.exp(a_log.astype(jnp.float32))
    chunk = SSM_CHUNK if t % SSM_CHUNK == 0 else t
    y, h_last = ssd(xs, dt, a, bm, cm, ssm_prev, chunk)
    y = y.astype(xs.dtype) + d_skip[:, None].astype(xs.dtype) * xs
    y = y.reshape(b, t, SSM_WIDTH) * jax.nn.silu(z)
    y = rms_norm(y.reshape(b, t, SSM_GROUPS, -1), norm_w.reshape(SSM_GROUPS, -1)).reshape(b, t, SSM_WIDTH)
    return y, h_last, new_conv


def mixer_sublayer(x, pos, past, conv_prev, ssm_prev, norm_w, w_in, conv_w, conv_b, dt_bias, a_log, d_skip,
                   ssm_norm_w, lam, lam_init, diff_norm_w, w_sb_out, w_ssm_out, w_diff_out, w_o):
    b, t, _ = x.shape
    h = rms_norm(x, norm_w)
    proj = jnp.einsum('btd,de->bte', h, w_in)
    (sb_q, sb_k, sb_v, z, xbc, dt_raw, d_q, d_k, d_v, g_sb, g_ssm, g_diff) = jnp.split(proj, _split_points(), axis=-1)
    sb_q = sb_q.reshape(b, t, SB_HEADS, SB_HEAD_DIM)
    sb_k = sb_k.reshape(b, t, SB_HEADS, SB_HEAD_DIM)
    sb_v = sb_v.reshape(b, t, SB_HEADS, SB_HEAD_DIM)
    d_q = rope(d_q.reshape(b, t, DIFF_HEADS, 2, DIFF_HEAD_DIM), pos)
    d_k = rope(d_k.reshape(b, t, DIFF_HEADS, 2, DIFF_HEAD_DIM), pos)
    d_k_rows = d_k.reshape(b, t, DIFF_HEADS, 2 * DIFF_HEAD_DIM)
    d_v = d_v.reshape(b, t, DIFF_HEADS, 2 * DIFF_HEAD_DIM)
    if past is None:
        o_sb = sweep_query_blocks(lambda qb, pb: sb_block(qb, sb_k, sb_v, pb, pos), sb_q, pos)
        o_diff = sweep_query_blocks(lambda qb, pb: diff_block(qb, d_k, d_v, lam, pb, pos), d_q, pos)
    else:
        p_sb_k, p_sb_v, p_d_k, p_d_v = past
        k_pos = jnp.arange(p_sb_k.shape[1] + t)
        o_sb = sb_block(sb_q, jnp.concatenate([p_sb_k, sb_k], 1), jnp.concatenate([p_sb_v, sb_v], 1), pos, k_pos)
        dk_all = jnp.concatenate([p_d_k, d_k_rows], 1).reshape(b, -1, DIFF_HEADS, 2, DIFF_HEAD_DIM)
        o_diff = diff_block(d_q, dk_all, jnp.concatenate([p_d_v, d_v], 1), lam, pos, k_pos)
    o_diff = rms_norm(o_diff, diff_norm_w) * (1.0 - lam_init)
    y_ssm, ssm_new, conv_new = mamba_branch(z, xbc, dt_raw, conv_prev, ssm_prev, conv_w, conv_b,
                                            dt_bias, a_log, d_skip, ssm_norm_w)
    merged = (jax.nn.sigmoid(g_sb) * (o_sb.reshape(b, t, SB_WIDTH) @ w_sb_out)
              + jax.nn.sigmoid(g_ssm) * (y_ssm @ w_ssm_out)
              + jax.nn.sigmoid(g_diff) * (o_diff.reshape(b, t, DIFF_WIDTH) @ w_diff_out))
    x = x + merged @ w_o
    return x, (sb_k, sb_v, d_k_rows, d_v), ssm_new, conv_new


def moe_sublayer(x, norm_w, w_group, w_expert, w_gate, w_up, w_down):
    b, t, d = x.shape
    h = rms_norm(x, norm_w).reshape(b * t, d)
    g_logits = (h @ w_group).astype(jnp.float32)
    g_sel = jnp.argmax(g_logits, axis=-1)
    g_w = jnp.take_along_axis(jax.nn.softmax(g_logits, axis=-1), g_sel[:, None], axis=-1)
    e_logits = (h @ w_expert).astype(jnp.float32).reshape(-1, MOE_GROUPS, MOE_EXPERTS_PER_GROUP)
    e_sel = jnp.take_along_axis(e_logits, g_sel[:, None, None], axis=1)[:, 0]
    top_v, top_i = lax.top_k(e_sel, MOE_TOP_K)
    top_w = jax.nn.softmax(top_v, axis=-1) * g_w
    expert_idx = g_sel[:, None] * MOE_EXPERTS_PER_GROUP + top_i
    combine = jnp.sum(jax.nn.one_hot(expert_idx, MOE_EXPERTS, dtype=jnp.float32) * top_w[..., None], axis=1)
    combine = combine.astype(h.dtype)
    y = jnp.zeros_like(h)
    for e in range(MOE_EXPERTS):
        he = jax.nn.silu(h @ w_gate[e]) * (h @ w_up[e])
        y = y + combine[:, e:e + 1] * (he @ w_down[e])
    return x + y.reshape(b, t, d)


def gather_pages(cache, page_table):
    g = cache[page_table]
    return g.reshape((g.shape[0], g.shape[1] * g.shape[2]) + g.shape[3:])


def setup_inputs(seed: int = 0) -> dict:
    key = jax.random.key(seed)
    ks = jax.random.split(key, 36)
    f32 = jnp.float32
    nrm = lambda k, shape, scale=1.0: scale * jax.random.normal(k, shape, f32)
    n_pages = PAST_LEN // PAGE_SIZE
    n_used = DEC_BATCH * n_pages
    n_phys = n_used + max(1, n_used // 4)
    sb_page = (DEPTH, n_phys, PAGE_SIZE, SB_HEADS, SB_HEAD_DIM)
    diff_page = (DEPTH, n_phys, PAGE_SIZE, DIFF_HEADS, 2 * DIFF_HEAD_DIM)
    page_table = jax.random.permutation(ks[8], n_phys)[:n_used].reshape(DEC_BATCH, n_pages).astype(jnp.int32)
    dt0 = jnp.exp(jax.random.uniform(ks[11], (DEPTH, SSM_HEADS), f32, math.log(1e-3), math.log(1e-1)))
    dt_bias = dt0 + jnp.log(-jnp.expm1(-dt0))
    return {
        'x_prompt': nrm(ks[0], (BATCH, SEQ, D_MODEL)),
        'x_sample': nrm(ks[1], (DEC_BATCH, DEC_SEQ, D_MODEL)),
        'cache_sb_k': nrm(ks[2], sb_page),
        'cache_sb_v': nrm(ks[3], sb_page),
        'cache_diff_k': nrm(ks[4], diff_page),
        'cache_diff_v': nrm(ks[5], diff_page),
        'state_ssm': nrm(ks[6], (DEPTH, DEC_BATCH, SSM_HEADS, SSM_HEAD_DIM, SSM_STATE), 0.1),
        'state_conv': nrm(ks[7], (DEPTH, DEC_BATCH, SSM_CONV - 1, SSM_CONV_DIM)),
        'page_table': page_table,
        'norm1_w': 1.0 + nrm(ks[9], (DEPTH, D_MODEL), 0.02),
        'w_in': nrm(ks[10], (DEPTH, D_MODEL, IN_DIM), D_MODEL ** -0.5),
        'conv_w': nrm(ks[12], (DEPTH, SSM_CONV, SSM_CONV_DIM), SSM_CONV ** -0.5),
        'conv_b': nrm(ks[13], (DEPTH, SSM_CONV_DIM), 0.02),
        'dt_bias': dt_bias,
        'a_log': jnp.log(jax.random.uniform(ks[14], (DEPTH, SSM_HEADS), f32, 1.0, 16.0)),
        'd_skip': 1.0 + nrm(ks[15], (DEPTH, SSM_HEADS), 0.02),
        'ssm_norm_w': 1.0 + nrm(ks[16], (DEPTH, SSM_WIDTH), 0.02),
        'lam_q1': nrm(ks[17], (DEPTH, DIFF_HEAD_DIM), 0.1),
        'lam_k1': nrm(ks[18], (DEPTH, DIFF_HEAD_DIM), 0.1),
        'lam_q2': nrm(ks[19], (DEPTH, DIFF_HEAD_DIM), 0.1),
        'lam_k2': nrm(ks[20], (DEPTH, DIFF_HEAD_DIM), 0.1),
        'diff_norm_w': 1.0 + nrm(ks[21], (DEPTH, 2 * DIFF_HEAD_DIM), 0.02),
        'w_sb_out': nrm(ks[22], (DEPTH, SB_WIDTH, D_MODEL), SB_WIDTH ** -0.5),
        'w_ssm_out': nrm(ks[23], (DEPTH, SSM_WIDTH, D_MODEL), SSM_WIDTH ** -0.5),
        'w_diff_out': nrm(ks[24], (DEPTH, DIFF_WIDTH, D_MODEL), DIFF_WIDTH ** -0.5),
        'w_o': nrm(ks[25], (DEPTH, D_MODEL, D_MODEL), D_MODEL ** -0.5),
        'norm2_w': 1.0 + nrm(ks[26], (DEPTH, D_MODEL), 0.02),
        'w_group': nrm(ks[27], (DEPTH, D_MODEL, MOE_GROUPS), D_MODEL ** -0.5),
        'w_expert': nrm(ks[28], (DEPTH, D_MODEL, MOE_EXPERTS), D_MODEL ** -0.5),
        'w_gate': nrm(ks[29], (DEPTH, MOE_EXPERTS, D_MODEL, MOE_FF), D_MODEL ** -0.5),
        'w_up': nrm(ks[30], (DEPTH, MOE_EXPERTS, D_MODEL, MOE_FF), D_MODEL ** -0.5),
        'w_down': nrm(ks[31], (DEPTH, MOE_EXPERTS, MOE_FF, D_MODEL), MOE_FF ** -0.5),
        'final_norm_w': 1.0 + nrm(ks[32], (D_MODEL,), 0.02),
    }


def reference(x_prompt, x_sample, cache_sb_k, cache_sb_v, cache_diff_k, cache_diff_v, state_ssm, state_conv,
              page_table, norm1_w, w_in, conv_w, conv_b, dt_bias, a_log, d_skip, ssm_norm_w,
              lam_q1, lam_k1, lam_q2, lam_k2, diff_norm_w, w_sb_out, w_ssm_out, w_diff_out, w_o,
              norm2_w, w_group, w_expert, w_gate, w_up, w_down, final_norm_w):
    f32 = jnp.float32
    b_p, t_p, _ = x_prompt.shape
    t_s = x_sample.shape[1]
    pos_p = jnp.arange(t_p)
    pos_s = PAST_LEN + jnp.arange(t_s)
    xp, xs = x_prompt, x_sample
    pk, pv, pdk, pdv, pssm, pconv = [], [], [], [], [], []
    sk, sv, sdk, sdv, sssm, sconv = [], [], [], [], [], []
    for l in range(DEPTH):
        lam_init = 0.8 - 0.6 * math.exp(-0.3 * l)
        lam = (jnp.exp(jnp.sum(lam_q1[l].astype(f32) * lam_k1[l].astype(f32)))
               - jnp.exp(jnp.sum(lam_q2[l].astype(f32) * lam_k2[l].astype(f32))) + lam_init)
        mix_w = (norm1_w[l], w_in[l], conv_w[l], conv_b[l], dt_bias[l], a_log[l], d_skip[l], ssm_norm_w[l],
                 lam, lam_init, diff_norm_w[l], w_sb_out[l], w_ssm_out[l], w_diff_out[l], w_o[l])
        conv0 = jnp.zeros((b_p, SSM_CONV - 1, SSM_CONV_DIM), xp.dtype)
        ssm0 = jnp.zeros((b_p, SSM_HEADS, SSM_HEAD_DIM, SSM_STATE), f32)
        xp, rows_p, ssm_p, conv_p = mixer_sublayer(xp, pos_p, None, conv0, ssm0, *mix_w)
        past = (gather_pages(cache_sb_k[l], page_table), gather_pages(cache_sb_v[l], page_table),
                gather_pages(cache_diff_k[l], page_table), gather_pages(cache_diff_v[l], page_table))
        xs, rows_s, ssm_s, conv_s = mixer_sublayer(xs, pos_s, past, state_conv[l], state_ssm[l], *mix_w)
        moe_w = (norm2_w[l], w_group[l], w_expert[l], w_gate[l], w_up[l], w_down[l])
        xp = moe_sublayer(xp, *moe_w)
        xs = moe_sublayer(xs, *moe_w)
        pk.append(rows_p[0]); pv.append(rows_p[1]); pdk.append(rows_p[2]); pdv.append(rows_p[3])
        pssm.append(ssm_p); pconv.append(conv_p)
        sk.append(rows_s[0]); sv.append(rows_s[1]); sdk.append(rows_s[2]); sdv.append(rows_s[3])
        sssm.append(ssm_s); sconv.append(conv_s)
    y_prompt = rms_norm(xp, final_norm_w)
    y_sample = rms_norm(xs, final_norm_w)
    return (y_prompt, y_sample,
            jnp.stack(pk), jnp.stack(pv), jnp.stack(pdk), jnp.stack(pdv), jnp.stack(pssm), jnp.stack(pconv),
            jnp.stack(sk), jnp.stack(sv), jnp.stack(sdk), jnp.stack(sdv), jnp.stack(sssm), jnp.stack(sconv))
```

```python
import functools
import math

import jax
import jax.numpy as jnp
from jax import lax
from jax.experimental import pallas as pl
from jax.experimental.pallas import tpu as pltpu

F32 = jnp.float32
BF16 = jnp.bfloat16

EPS = 1e-6
ROPE_THETA = 10000.0
LANES = 128
HEAD_DIM = 64
SSM_STATE = 64
SSM_CHUNK = 128
CONV_TAPS = 4
MOE_GROUPS = 4
MOE_PER_GROUP = 4
VMEM_LIMIT = 48 * 1024 * 1024

D_MODEL = 1024
W_BRANCH = 512
C_GATE = 0
C_SBQ = 3072
C_SBK = 3584
C_SBV = 4096
C_Z = 4608
C_XBC = 5120
C_DT = 5888
C_DQ = 6144
C_DK = 6656
C_DV = 7168
PROJ_W = 7680
COL_TILE = 512


def _cparams(sem):
    return pltpu.CompilerParams(dimension_semantics=sem, vmem_limit_bytes=VMEM_LIMIT)


def _hi_lo(a):
    hi = a.astype(BF16)
    lo = (a - hi.astype(F32)).astype(BF16)
    return hi, lo


_NN = (((1,), (0,)), ((), ()))
_NT = (((1,), (1,)), ((), ()))
_TN = (((0,), (0,)), ((), ()))


def _dg(a, b, dn):
    return lax.dot_general(a, b, dn, preferred_element_type=F32)


def _mm(a, b, precise, dn=_NN):
    if not precise:
        return _dg(a.astype(BF16), b.astype(BF16), dn)
    ah, al = _hi_lo(a.astype(F32))
    bh, bl = _hi_lo(b.astype(F32))
    return _dg(ah, bh, dn) + _dg(ah, bl, dn) + _dg(al, bh, dn)


def _mm_exact_rhs(a, b_exact, dn=_NN):
    ah, al = _hi_lo(a)
    return _dg(ah, b_exact, dn) + _dg(al, b_exact, dn)


def _mm_exact_lhs(a_exact, b, dn=_NN):
    bh, bl = _hi_lo(b)
    return _dg(a_exact, bh, dn) + _dg(a_exact, bl, dn)


def _softplus(z):
    return jnp.maximum(z, 0.0) + jnp.log(1.0 + jnp.exp(-jnp.abs(z)))


def _rms(x, w):
    return x * lax.rsqrt(jnp.mean(x * x, axis=-1, keepdims=True) + EPS) * w


def _inproj_kernel(x_ref, nw_ref, w_ref, cos_ref, sin_ref, o_ref, hh_ref, *maybe_hl, precise):
    j = pl.program_id(1)

    @pl.when(j == 0)
    def _():
        h = _rms(x_ref[...], nw_ref[...])
        hi = h.astype(BF16)
        hh_ref[...] = hi
        if precise:
            maybe_hl[0][...] = (h - hi.astype(F32)).astype(BF16)

    if precise:
        wh, wl = _hi_lo(w_ref[...])
        hh = hh_ref[...]
        acc = _dg(hh, wh, _NN) + _dg(hh, wl, _NN) + _dg(maybe_hl[0][...], wh, _NN)
    else:
        acc = _dg(hh_ref[...], w_ref[...], _NN)

    is_rope = jnp.logical_or(j == C_DQ // COL_TILE, j == C_DK // COL_TILE)

    @pl.when(is_rope)
    def _():
        lane = lax.broadcasted_iota(jnp.int32, acc.shape, 1)
        first = (lane % HEAD_DIM) < (HEAD_DIM // 2)
        partner = jnp.where(first, pltpu.roll(acc, COL_TILE - HEAD_DIM // 2, 1),
                            pltpu.roll(acc, HEAD_DIM // 2, 1))
        o_ref[...] = acc * cos_ref[...] + partner * sin_ref[...]

    @pl.when(jnp.logical_not(is_rope))
    def _():
        o_ref[...] = acc


def _inproj(x, norm_w, w, cos_t, sin_t, *, precise, tm):
    t = x.shape[0]
    n_pos_tiles = cos_t.shape[0] // tm
    scratch = [pltpu.VMEM((tm, D_MODEL), BF16)]
    if precise:
        scratch.append(pltpu.VMEM((tm, D_MODEL), BF16))
    return pl.pallas_call(
        functools.partial(_inproj_kernel, precise=precise),
        grid=(t // tm, PROJ_W // COL_TILE),
        in_specs=[
            pl.BlockSpec((tm, D_MODEL), lambda i, j: (i, 0)),
            pl.BlockSpec((1, D_MODEL), lambda i, j: (0, 0)),
            pl.BlockSpec((D_MODEL, COL_TILE), lambda i, j: (0, j)),
            pl.BlockSpec((tm, COL_TILE), lambda i, j: (i % n_pos_tiles, 0)),
            pl.BlockSpec((tm, COL_TILE), lambda i, j: (i % n_pos_tiles, 0)),
        ],
        out_specs=pl.BlockSpec((tm, COL_TILE), lambda i, j: (i, j)),
        out_shape=jax.ShapeDtypeStruct((t, PROJ_W), F32),
        scratch_shapes=scratch,
        compiler_params=_cparams(("parallel", "arbitrary")),
    )(x, norm_w, w, cos_t, sin_t)


def _sb_prompt_kernel(q_ref, k_ref, v_ref, o_ref, acc_ref, *, tq):
    qi = pl.program_id(2)
    q = q_ref[...] * (1.0 / math.sqrt(HEAD_DIM))
    lane_q = lax.broadcasted_iota(jnp.int32, (tq, LANES), 1)
    q_heads = (jnp.where(lane_q < HEAD_DIM, q, 0.0).astype(BF16),
               jnp.where(lane_q >= HEAD_DIM, q, 0.0).astype(BF16))
    row = lax.broadcasted_iota(jnp.int32, (tq, tq), 0)
    col = lax.broadcasted_iota(jnp.int32, (tq, tq), 1)
    later = jnp.where(row > col, 1.0, 0.0).astype(BF16)
    strictly_causal = col < row
    acc_ref[...] = jnp.zeros_like(acc_ref)

    def block(kb, carry, diagonal):
        start = pl.multiple_of(kb * tq, tq)
        k = k_ref[pl.ds(start, tq), :].astype(BF16)
        v = v_ref[pl.ds(start, tq), :]
        v_heads = (jnp.where(lane_q < HEAD_DIM, v, 0.0).astype(BF16),
                   jnp.where(lane_q >= HEAD_DIM, v, 0.0).astype(BF16))
        new_carry = []
        contrib = None
        for hd in range(2):
            z = _dg(q_heads[hd], k, _NT)
            sp = _softplus(z)
            if diagonal:
                sp = jnp.where(strictly_causal, sp, 0.0)
            after = _mm_exact_rhs(sp, later)
            total = after[:, 0:1] + sp[:, 0:1]
            w = jnp.exp(z - sp - after - carry[hd])
            if diagonal:
                w = jnp.where(strictly_causal, w, 0.0)
            pv = _dg(w.astype(BF16), v_heads[hd], _NN)
            contrib = pv if contrib is None else contrib + pv
            new_carry.append(carry[hd] + total)
        acc_ref[...] += contrib
        return tuple(new_carry)

    zero = jnp.zeros((tq, 1), F32)
    carry = block(qi, (zero, zero), True)
    lax.fori_loop(0, qi, lambda it, c: block(qi - 1 - it, c, False), carry)
    o_ref[...] = acc_ref[...].astype(o_ref.dtype)


def _sb_prompt(proj, *, batch, seq, tq):
    nq = seq // tq
    t = batch * seq
    cq, ck, cv = C_SBQ // LANES, C_SBK // LANES, C_SBV // LANES
    return pl.pallas_call(
        functools.partial(_sb_prompt_kernel, tq=tq),
        grid=(batch, W_BRANCH // LANES, nq),
        in_specs=[
            pl.BlockSpec((tq, LANES), lambda b, hp, qi: (b * nq + qi, cq + hp)),
            pl.BlockSpec((seq, LANES), lambda b, hp, qi: (b, ck + hp)),
            pl.BlockSpec((seq, LANES), lambda b, hp, qi: (b, cv + hp)),
        ],
        out_specs=pl.BlockSpec((tq, LANES), lambda b, hp, qi: (b * nq + qi, hp)),
        out_shape=jax.ShapeDtypeStruct((t, W_BRANCH), BF16),
        scratch_shapes=[pltpu.VMEM((tq, LANES), F32)],
        compiler_params=_cparams(("parallel", "parallel", "arbitrary")),
    )(proj, proj, proj)


def _lambda_value(lam_ref, lam_init):
    lv = lam_ref[...]
    s1 = jnp.sum(lv[0:1, :] * lv[1:2, :], axis=-1, keepdims=True)
    s2 = jnp.sum(lv[2:3, :] * lv[3:4, :], axis=-1, keepdims=True)
    return jnp.exp(s1) - jnp.exp(s2) + lam_init


def _diff_prompt_kernel(q_ref, k_ref, v_ref, lam_ref, nw_ref, o_ref, acc_ref, ml_ref, *, tq, lam_init):
    qi = pl.program_id(2)
    q = q_ref[...] * (1.0 / math.sqrt(HEAD_DIM))
    lane_q = lax.broadcasted_iota(jnp.int32, (tq, LANES), 1)
    q_comp = (jnp.where(lane_q < HEAD_DIM, q, 0.0).astype(BF16),
              jnp.where(lane_q >= HEAD_DIM, q, 0.0).astype(BF16))
    row = lax.broadcasted_iota(jnp.int32, (tq, tq), 0)
    col = lax.broadcasted_iota(jnp.int32, (tq, tq), 1)
    causal = col <= row

    def block(kb, diagonal, first):
        start = pl.multiple_of(kb * tq, tq)
        k = k_ref[pl.ds(start, tq), :].astype(BF16)
        v = v_ref[pl.ds(start, tq), :].astype(BF16)
        for c in range(2):
            s = _dg(q_comp[c], k, _NT)
            if diagonal:
                s = jnp.where(causal, s, -jnp.inf)
            blk_max = jnp.max(s, axis=-1, keepdims=True)
            if first:
                m_new = blk_max
                p = jnp.exp(s - m_new)
                ml_ref[2 * c] = m_new
                ml_ref[2 * c + 1] = jnp.sum(p, axis=-1, keepdims=True)
                acc_ref[c] = _dg(p.astype(BF16), v, _NN)
            else:
                m_old = ml_ref[2 * c]
                m_new = jnp.maximum(m_old, blk_max)
                alpha = jnp.exp(m_old - m_new)
                p = jnp.exp(s - m_new)
                ml_ref[2 * c] = m_new
                ml_ref[2 * c + 1] = alpha * ml_ref[2 * c + 1] + jnp.sum(p, axis=-1, keepdims=True)
                acc_ref[c] = alpha * acc_ref[c] + _dg(p.astype(BF16), v, _NN)

    block(qi, True, True)

    def body(it, _):
        block(it, False, False)
        return 0

    lax.fori_loop(0, qi, body, 0)
    lam = _lambda_value(lam_ref, lam_init)
    o = acc_ref[0] / ml_ref[1] - lam * (acc_ref[1] / ml_ref[3])
    o_ref[...] = (_rms(o, nw_ref[...]) * (1.0 - lam_init)).astype(o_ref.dtype)


def _diff_prompt(proj, lam_vecs, diff_norm_w, *, batch, seq, tq, lam_init):
    nq = seq // tq
    t = batch * seq
    cq, ck, cv = C_DQ // LANES, C_DK // LANES, C_DV // LANES
    return pl.pallas_call(
        functools.partial(_diff_prompt_kernel, tq=tq, lam_init=lam_init),
        grid=(batch, W_BRANCH // LANES, nq),
        in_specs=[
            pl.BlockSpec((tq, LANES), lambda b, hh, qi: (b * nq + qi, cq + hh)),
            pl.BlockSpec((seq, LANES), lambda b, hh, qi: (b, ck + hh)),
            pl.BlockSpec((seq, LANES), lambda b, hh, qi: (b, cv + hh)),
            pl.BlockSpec((4, HEAD_DIM), lambda b, hh, qi: (0, 0)),
            pl.BlockSpec((1, LANES), lambda b, hh, qi: (0, 0)),
        ],
        out_specs=pl.BlockSpec((tq, LANES), lambda b, hh, qi: (b * nq + qi, hh)),
        out_shape=jax.ShapeDtypeStruct((t, W_BRANCH), BF16),
        scratch_shapes=[pltpu.VMEM((2, tq, LANES), F32), pltpu.VMEM((4, tq, 1), F32)],
        compiler_params=_cparams(("parallel", "parallel", "arbitrary")),
    )(proj, proj, proj, lam_vecs, diff_norm_w)


def _ssd_prompt_kernel(z_ref, xa_ref, xb_ref, cw_ref, cb_ref, dtb_ref, alog_ref, dsk_ref, nw_ref,
                       y_ref, hout_ref, cout_ref, xpad_ref, h_ref, *, n_heads, n_groups):
    c = pl.program_id(1)
    nc = pl.num_programs(1)
    L = SSM_CHUNK
    P = HEAD_DIM
    N = SSM_STATE
    xw = n_heads * P
    gn = n_groups * N

    @pl.when(c == 0)
    def _():
        xpad_ref[0:8, :] = jnp.zeros((8, xpad_ref.shape[1]), F32)
        h_ref[...] = jnp.zeros_like(h_ref)

    xb = xb_ref[...]
    xbc = jnp.concatenate([xa_ref[...], xb[:, :2 * gn]], axis=1)
    xpad_ref[8:8 + L, :] = xbc
    conv = cb_ref[...]
    for tap in range(CONV_TAPS):
        conv = conv + xpad_ref[pl.ds(8 - (CONV_TAPS - 1) + tap, L), :] * cw_ref[tap:tap + 1, :]
    conv = conv * jax.nn.sigmoid(conv)
    xpad_ref[0:8, :] = xbc[L - 8:, :]

    @pl.when(c == nc - 1)
    def _():
        cout_ref[0] = xbc[L - (CONV_TAPS - 1):, :]

    xs = conv[:, :xw]
    bmat = conv[:, xw:xw + gn]
    cmat = conv[:, xw + gn:]

    lane = lax.broadcasted_iota(jnp.int32, (1, LANES), 1)
    head_lane = lane < n_heads
    dt = jnp.where(head_lane, _softplus(xb[:, 2 * gn:2 * gn + LANES] + dtb_ref[...]), 0.0)
    a = jnp.where(head_lane, -jnp.exp(alog_ref[...]), 0.0)
    dta = dt * a
    ri = lax.broadcasted_iota(jnp.int32, (L, L), 0)
    ci = lax.broadcasted_iota(jnp.int32, (L, L), 1)
    lower = jnp.where(ri >= ci, 1.0, 0.0).astype(BF16)
    dta_h, dta_l = _hi_lo(dta)
    dta_l2 = (dta - dta_h.astype(F32) - dta_l.astype(F32)).astype(BF16)
    cum = _dg(lower, dta_h, _NN) + _dg(lower, dta_l, _NN) + _dg(lower, dta_l2, _NN)
    cum_t = cum.T
    dt_t = dt.T
    causal = ri >= ci

    y_parts = []
    for g in range(n_groups):
        b_g = bmat[:, g * N:(g + 1) * N]
        c_g = cmat[:, g * N:(g + 1) * N]
        cb = _mm(c_g, b_g, False, _NT)
        for hh in range(n_heads // n_groups):
            hd = g * (n_heads // n_groups) + hh
            x_h = xs[:, hd * P:(hd + 1) * P]
            cum_col = cum[:, hd:hd + 1]
            cum_row = cum_t[hd:hd + 1, :]
            dt_col = dt[:, hd:hd + 1]
            dt_row = dt_t[hd:hd + 1, :]
            decay = jnp.exp(jnp.where(causal, cum_col - cum_row, -jnp.inf))
            y_intra = _mm(cb * decay * dt_row, x_h, False)
            h_prev = h_ref[hd]
            y_inter = _mm(c_g, h_prev, False, _NT) * jnp.exp(cum_col)
            cum_end = cum[L - 1:L, hd:hd + 1]
            w_end = jnp.exp(cum_end - cum_col) * dt_col
            s_chunk = _mm(x_h * w_end, b_g, False, _TN)
            h_ref[hd] = jnp.exp(cum_end) * h_prev + s_chunk
            y_parts.append(y_intra + y_inter + dsk_ref[0:1, hd:hd + 1] * x_h)
    y = jnp.concatenate(y_parts, axis=1)
    zg = z_ref[...]
    y = y * (zg * jax.nn.sigmoid(zg))
    gw = xw // n_groups
    nw = nw_ref[...]
    y = jnp.concatenate([_rms(y[:, g * gw:(g + 1) * gw], nw[:, g * gw:(g + 1) * gw])
                         for g in range(n_groups)], axis=1)
    y_ref[...] = y.astype(y_ref.dtype)

    @pl.when(c == nc - 1)
    def _():
        hout_ref[0] = h_ref[...]


def _ssd_prompt(proj, conv_w, conv_b, dt_bias, a_log, d_skip, norm_w, *, batch, seq, n_heads, n_groups):
    L = SSM_CHUNK
    nc = seq // L
    t = batch * seq
    cdim = conv_w.shape[1]
    cz, cx = C_Z // COL_TILE, C_XBC // COL_TILE
    full = lambda shape: pl.BlockSpec(shape, lambda b, c: (0,) * len(shape))
    return pl.pallas_call(
        functools.partial(_ssd_prompt_kernel, n_heads=n_heads, n_groups=n_groups),
        grid=(batch, nc),
        in_specs=[
            pl.BlockSpec((L, COL_TILE), lambda b, c: (b * nc + c, cz)),
            pl.BlockSpec((L, COL_TILE), lambda b, c: (b * nc + c, cx)),
            pl.BlockSpec((L, COL_TILE), lambda b, c: (b * nc + c, cx + 1)),
            full((CONV_TAPS, cdim)), full((1, cdim)), full((1, LANES)), full((1, LANES)),
            full((1, LANES)), full((1, W_BRANCH)),
        ],
        out_specs=[
            pl.BlockSpec((L, W_BRANCH), lambda b, c: (b * nc + c, 0)),
            pl.BlockSpec((1, n_heads, HEAD_DIM, SSM_STATE), lambda b, c: (b, 0, 0, 0)),
            pl.BlockSpec((1, CONV_TAPS - 1, cdim), lambda b, c: (b, 0, 0)),
        ],
        out_shape=[
            jax.ShapeDtypeStruct((t, W_BRANCH), BF16),
            jax.ShapeDtypeStruct((batch, n_heads, HEAD_DIM, SSM_STATE), F32),
            jax.ShapeDtypeStruct((batch, CONV_TAPS - 1, cdim), F32),
        ],
        scratch_shapes=[pltpu.VMEM((8 + L, cdim), F32), pltpu.VMEM((n_heads, HEAD_DIM, SSM_STATE), F32)],
        compiler_params=_cparams(("parallel", "arbitrary")),
    )(proj, proj, proj, conv_w, conv_b, dt_bias, a_log, d_skip, norm_w)


def _merge_kernel(x_ref, g0_ref, g1_ref, g2_ref, a_ref, b_ref, c_ref, wa_ref, wb_ref, wc_ref, wo_ref,
                  o_ref, *, precise):
    m = (jax.nn.sigmoid(g0_ref[...]) * _mm(a_ref[...], wa_ref[...], precise)
         + jax.nn.sigmoid(g1_ref[...]) * _mm(b_ref[...], wb_ref[...], precise)
         + jax.nn.sigmoid(g2_ref[...]) * _mm(c_ref[...], wc_ref[...], precise))
    o_ref[...] = x_ref[...] + _mm(m, wo_ref[...], precise)


def _merge(x, proj, o_sb, y_ssm, o_diff, w_sb, w_ssm, w_diff, w_o, *, precise, tm):
    t = x.shape[0]
    full = lambda shape: pl.BlockSpec(shape, lambda i: (0, 0))
    gate = lambda n: pl.BlockSpec((tm, D_MODEL), lambda i: (i, C_GATE // D_MODEL + n))
    branch = pl.BlockSpec((tm, W_BRANCH), lambda i: (i, 0))
    return pl.pallas_call(
        functools.partial(_merge_kernel, precise=precise),
        grid=(t // tm,),
        in_specs=[pl.BlockSpec((tm, D_MODEL), lambda i: (i, 0)), gate(0), gate(1), gate(2),
                  branch, branch, branch,
                  full((W_BRANCH, D_MODEL)), full((W_BRANCH, D_MODEL)), full((W_BRANCH, D_MODEL)),
                  full((D_MODEL, D_MODEL))],
        out_specs=pl.BlockSpec((tm, D_MODEL), lambda i: (i, 0)),
        out_shape=jax.ShapeDtypeStruct((t, D_MODEL), F32),
        compiler_params=_cparams(("parallel",)),
    )(x, proj, proj, proj, o_sb, y_ssm, o_diff, w_sb, w_ssm, w_diff, w_o)


def _route(logits):
    lane = lax.broadcasted_iota(jnp.int32, logits.shape, 1)
    neg = -jnp.inf
    big = jnp.int32(1 << 20)
    g = jnp.where(lane < MOE_GROUPS, logits, neg)
    g_max = jnp.max(g, axis=-1, keepdims=True)
    g_sel = jnp.min(jnp.where(g == g_max, lane, big), axis=-1, keepdims=True)
    g_w = 1.0 / jnp.sum(jnp.exp(g - g_max), axis=-1, keepdims=True)
    lo = MOE_GROUPS + g_sel * MOE_PER_GROUP
    e = jnp.where((lane >= lo) & (lane < lo + MOE_PER_GROUP), logits, neg)
    v1 = jnp.max(e, axis=-1, keepdims=True)
    i1 = jnp.min(jnp.where(e == v1, lane, big), axis=-1, keepdims=True)
    e2 = jnp.where(lane == i1, neg, e)
    v2 = jnp.max(e2, axis=-1, keepdims=True)
    i2 = jnp.min(jnp.where(e2 == v2, lane, big), axis=-1, keepdims=True)
    r = jnp.exp(v2 - v1)
    w1 = g_w / (1.0 + r)
    w2 = g_w * r / (1.0 + r)
    return jnp.where(lane == i1, w1, 0.0) + jnp.where(lane == i2, w2, 0.0)


def _moe_kernel(x_ref, nw_ref, wr_ref, wg_ref, wu_ref, wd_ref, fw_ref, o_ref,
                hh_ref, hl_ref, comb_ref, acc_ref, *, precise, final):
    e = pl.program_id(1)
    ne = pl.num_programs(1)

    @pl.when(e == 0)
    def _():
        h = _rms(x_ref[...], nw_ref[...])
        hi = h.astype(BF16)
        lo = (h - hi.astype(F32)).astype(BF16)
        hh_ref[...] = hi
        hl_ref[...] = lo
        wrh, wrl = _hi_lo(wr_ref[...])
        logits = _dg(hi, wrh, _NN) + _dg(hi, wrl, _NN) + _dg(lo, wrh, _NN)
        comb_ref[...] = _route(logits)
        acc_ref[...] = jnp.zeros_like(acc_ref)

    hh = hh_ref[...]
    if precise:
        hl = hl_ref[...]
        gh, gl = _hi_lo(wg_ref[0])
        uh, ul = _hi_lo(wu_ref[0])
        gate = _dg(hh, gh, _NN) + _dg(hh, gl, _NN) + _dg(hl, gh, _NN)
        up = _dg(hh, uh, _NN) + _dg(hh, ul, _NN) + _dg(hl, uh, _NN)
    else:
        gate = _dg(hh, wg_ref[0], _NN)
        up = _dg(hh, wu_ref[0], _NN)
    lane = lax.broadcasted_iota(jnp.int32, comb_ref.shape, 1)
    cw = jnp.sum(jnp.where(lane == e + MOE_GROUPS, comb_ref[...], 0.0), axis=-1, keepdims=True)
    he = gate * jax.nn.sigmoid(gate) * up * cw
    acc_ref[...] += _mm(he, wd_ref[0], precise)

    @pl.when(e == ne - 1)
    def _():
        y = x_ref[...] + acc_ref[...]
        if final:
            y = _rms(y, fw_ref[...])
        o_ref[...] = y


def _moe(x, norm_w, w_route, w_gate, w_up, w_down, final_w, *, precise, final, tm):
    t = x.shape[0]
    n_exp, _, ff = w_gate.shape
    return pl.pallas_call(
        functools.partial(_moe_kernel, precise=precise, final=final),
        grid=(t // tm, n_exp),
        in_specs=[
            pl.BlockSpec((tm, D_MODEL), lambda i, e: (i, 0)),
            pl.BlockSpec((1, D_MODEL), lambda i, e: (0, 0)),
            pl.BlockSpec((D_MODEL, LANES), lambda i, e: (0, 0)),
            pl.BlockSpec((1, D_MODEL, ff), lambda i, e: (e, 0, 0)),
            pl.BlockSpec((1, D_MODEL, ff), lambda i, e: (e, 0, 0)),
            pl.BlockSpec((1, ff, D_MODEL), lambda i, e: (e, 0, 0)),
            pl.BlockSpec((1, D_MODEL), lambda i, e: (0, 0)),
        ],
        out_specs=pl.BlockSpec((tm, D_MODEL), lambda i, e: (i, 0)),
        out_shape=jax.ShapeDtypeStruct((t, D_MODEL), F32),
        scratch_shapes=[pltpu.VMEM((tm, D_MODEL), BF16), pltpu.VMEM((tm, D_MODEL), BF16),
                        pltpu.VMEM((tm, LANES), F32), pltpu.VMEM((tm, D_MODEL), F32)],
        compiler_params=_cparams(("parallel", "arbitrary")),
    )(x, norm_w, w_route, w_gate, w_up, w_down, final_w)


def _row_to_col(row, eye):
    return jnp.sum(jnp.where(eye, jnp.broadcast_to(row, eye.shape), 0.0), axis=1, keepdims=True)


def _col_to_row(col, eye):
    return jnp.sum(jnp.where(eye, jnp.broadcast_to(col, eye.shape), 0.0), axis=0, keepdims=True)


def _sb_sample_kernel(pt_ref, q_ref, *refs, pages_per_step, n_heads):
    k_refs = refs[:pages_per_step]
    v_refs = refs[pages_per_step:2 * pages_per_step]
    o_ref, qb_ref, acc_ref, c_ref = refs[2 * pages_per_step:]
    j = pl.program_id(1)
    nj = pl.num_programs(1)
    P = HEAD_DIM
    page = qb_ref.shape[-1]
    eye = lax.broadcasted_iota(jnp.int32, (P, P), 0) == lax.broadcasted_iota(jnp.int32, (P, P), 1)

    @pl.when(j == 0)
    def _():
        q = q_ref[0] * (1.0 / math.sqrt(HEAD_DIM))
        for hd in range(n_heads):
            qb_ref[hd] = jnp.broadcast_to(_row_to_col(q[:, hd * P:(hd + 1) * P], eye), (P, page))
        acc_ref[...] = jnp.zeros_like(acc_ref)
        c_ref[...] = jnp.zeros_like(c_ref)

    ri = lax.broadcasted_iota(jnp.int32, (page, page), 0)
    ci = lax.broadcasted_iota(jnp.int32, (page, page), 1)
    later = jnp.where(ri > ci, 1.0, 0.0).astype(BF16)
    carry = c_ref[:, 0:1]
    for p in range(pages_per_step):
        z = jnp.concatenate([jnp.sum(k_refs[p][0, 0, hd] * qb_ref[hd], axis=0, keepdims=True)
                             for hd in range(n_heads)], axis=0)
        sp = _softplus(z)
        sph, spl = _hi_lo(sp)
        spl2 = (sp - sph.astype(F32) - spl.astype(F32)).astype(BF16)
        after = _dg(sph, later, _NN) + _dg(spl, later, _NN) + _dg(spl2, later, _NN)
        w = jnp.exp(z - sp - after - carry)
        for hd in range(n_heads):
            acc_ref[hd] += v_refs[p][0, 0, hd] * w[hd:hd + 1, :]
        carry = carry + after[:, 0:1] + sp[:, 0:1]
    c_ref[...] = jnp.broadcast_to(carry, c_ref.shape)

    @pl.when(j == nj - 1)
    def _():
        o_ref[0] = jnp.concatenate(
            [_col_to_row(jnp.sum(acc_ref[hd], axis=1, keepdims=True), eye) for hd in range(n_heads)], axis=1)


def _sb_sample(page_table, q, cache_k_t, cache_v_t, layer, *, pages_per_step):
    nb, n_pages = page_table.shape
    _, _, n_heads, hdim, page = cache_k_t.shape
    width = n_heads * hdim
    nj = n_pages // pages_per_step
    pt_flat = page_table.reshape(-1)

    def page_spec(p):
        def index(b, j, pt):
            return (layer, pt[b * n_pages + (nj - 1 - j) * pages_per_step + (pages_per_step - 1 - p)], 0, 0, 0)
        return pl.BlockSpec((1, 1, n_heads, hdim, page), index)

    specs = [page_spec(p) for p in range(pages_per_step)]
    grid_spec = pltpu.PrefetchScalarGridSpec(
        num_scalar_prefetch=1,
        grid=(nb, nj),
        in_specs=[pl.BlockSpec((1, 1, width), lambda b, j, pt: (b, 0, 0))] + specs + specs,
        out_specs=pl.BlockSpec((1, 1, width), lambda b, j, pt: (b, 0, 0)),
        scratch_shapes=[pltpu.VMEM((n_heads, hdim, page), F32), pltpu.VMEM((n_heads, hdim, page), F32),
                        pltpu.VMEM((n_heads, LANES), F32)],
    )
    return pl.pallas_call(
        functools.partial(_sb_sample_kernel, pages_per_step=pages_per_step, n_heads=n_heads),
        grid_spec=grid_spec,
        out_shape=jax.ShapeDtypeStruct((nb, 1, width), F32),
        compiler_params=_cparams(("parallel", "arbitrary")),
    )(pt_flat, q, *([cache_k_t] * pages_per_step), *([cache_v_t] * pages_per_step))


def _diff_sample_kernel(pt_ref, q_ref, kn_ref, vn_ref, lam_ref, nw_ref, *refs, pages_per_step, n_heads, lam_init):
    k_refs = refs[:pages_per_step]
    v_refs = refs[pages_per_step:2 * pages_per_step]
    o_ref, q8_ref, acc_ref, m_ref, l_ref = refs[2 * pages_per_step:]
    j = pl.program_id(1)
    nj = pl.num_programs(1)
    rows = k_refs[0].shape[2]
    comp_of_lane = lax.broadcasted_iota(jnp.int32, (LANES, LANES), 0) // HEAD_DIM
    sel = [jnp.where(comp_of_lane == c, 1.0, 0.0).astype(BF16) for c in range(2)]
    row8 = lax.broadcasted_iota(jnp.int32, (8, LANES), 0)

    def per_sublane(vec):
        return jnp.concatenate([vec[:, (i % n_heads) * LANES:(i % n_heads + 1) * LANES] for i in range(8)], axis=0)

    @pl.when(j == 0)
    def _():
        q8 = per_sublane(q_ref[0]) * (1.0 / math.sqrt(HEAD_DIM))
        q8_ref[...] = q8
        prod = per_sublane(kn_ref[0]) * q8
        v_new = jnp.where(row8 < n_heads, per_sublane(vn_ref[0]), 0.0)
        for c in range(2):
            m_ref[c] = _mm_exact_rhs(prod, sel[c])
            l_ref[c] = jnp.where(row8 < n_heads, 1.0, 0.0)
            acc_ref[c] = v_new

    q8 = q8_ref[...]
    for p in range(pages_per_step):
        k3 = k_refs[p][0, 0].reshape(rows // 8, 8, LANES)
        v3 = v_refs[p][0, 0].reshape(rows // 8, 8, LANES)
        prod = (k3 * q8[None]).reshape(rows, LANES)
        for c in range(2):
            s = _mm_exact_rhs(prod, sel[c]).reshape(rows // 8, 8, LANES)
            blk_max = jnp.max(s, axis=0)
            blk_max = jnp.maximum(blk_max, pltpu.roll(blk_max, n_heads, 0))
            m_old = m_ref[c]
            m_new = jnp.maximum(m_old, blk_max)
            alpha = jnp.exp(m_old - m_new)
            pr = jnp.exp(s - m_new[None])
            l_ref[c] = alpha * l_ref[c] + jnp.sum(pr, axis=0)
            acc_ref[c] = alpha * acc_ref[c] + jnp.sum(pr * v3, axis=0)
            m_ref[c] = m_new

    @pl.when(j == nj - 1)
    def _():
        lam = _lambda_value(lam_ref, lam_init)
        outs = []
        for c in range(2):
            l = l_ref[c]
            a = acc_ref[c]
            outs.append((a + pltpu.roll(a, n_heads, 0)) / (l + pltpu.roll(l, n_heads, 0)))
        o = _rms(outs[0] - lam * outs[1], nw_ref[...]) * (1.0 - lam_init)
        o_ref[0] = jnp.concatenate([o[hd:hd + 1, :] for hd in range(n_heads)], axis=1)


def _diff_sample(page_table, q, k_new, v_new, lam_vecs, diff_norm_w, cache_k_r, cache_v_r, layer, *,
                 pages_per_step, n_heads, lam_init):
    nb, n_pages = page_table.shape
    rows = cache_k_r.shape[2]
    width = n_heads * LANES
    nj = n_pages // pages_per_step
    pt_flat = page_table.reshape(-1)

    def page_spec(p):
        def index(b, j, pt):
            return (layer, pt[b * n_pages + j * pages_per_step + p], 0, 0)
        return pl.BlockSpec((1, 1, rows, LANES), index)

    specs = [page_spec(p) for p in range(pages_per_step)]
    row = pl.BlockSpec((1, 1, width), lambda b, j, pt: (b, 0, 0))
    grid_spec = pltpu.PrefetchScalarGridSpec(
        num_scalar_prefetch=1,
        grid=(nb, nj),
        in_specs=[row, row, row,
                  pl.BlockSpec((4, HEAD_DIM), lambda b, j, pt: (0, 0)),
                  pl.BlockSpec((1, LANES), lambda b, j, pt: (0, 0))] + specs + specs,
        out_specs=row,
        scratch_shapes=[pltpu.VMEM((8, LANES), F32), pltpu.VMEM((2, 8, LANES), F32),
                        pltpu.VMEM((2, 8, LANES), F32), pltpu.VMEM((2, 8, LANES), F32)],
    )
    return pl.pallas_call(
        functools.partial(_diff_sample_kernel, pages_per_step=pages_per_step, n_heads=n_heads, lam_init=lam_init),
        grid_spec=grid_spec,
        out_shape=jax.ShapeDtypeStruct((nb, 1, width), F32),
        compiler_params=_cparams(("parallel", "arbitrary")),
    )(pt_flat, q, k_new, v_new, lam_vecs, diff_norm_w,
      *([cache_k_r] * pages_per_step), *([cache_v_r] * pages_per_step))


def _ssd_sample_kernel(z_ref, xa_ref, xb_ref, cprev_ref, hprev_ref, cw_ref, cb_ref, dtb_ref, alog_ref, dsk_ref,
                       nw_ref, y_ref, hout_ref, cout_ref, *, n_heads, n_groups):
    P = HEAD_DIM
    N = SSM_STATE
    xw = n_heads * P
    gn = n_groups * N
    xb = xb_ref[0]
    xbc = jnp.concatenate([xa_ref[0], xb[:, :2 * gn]], axis=1)
    prev = cprev_ref[0]
    conv = cb_ref[...] + xbc * cw_ref[CONV_TAPS - 1:CONV_TAPS, :]
    for tap in range(CONV_TAPS - 1):
        conv = conv + prev[tap:tap + 1, :] * cw_ref[tap:tap + 1, :]
    conv = conv * jax.nn.sigmoid(conv)
    cout_ref[0, 0:CONV_TAPS - 2, :] = prev[1:, :]
    cout_ref[0, CONV_TAPS - 2:CONV_TAPS - 1, :] = xbc
    xs = conv[:, :xw]
    dt = _softplus(xb[:, 2 * gn:2 * gn + LANES] + dtb_ref[...])
    a = -jnp.exp(alog_ref[...])
    decay = jnp.exp(dt * a)
    eye = lax.broadcasted_iota(jnp.int32, (P, P), 0) == lax.broadcasted_iota(jnp.int32, (P, P), 1)
    y_parts = []
    for hd in range(n_heads):
        g = hd // (n_heads // n_groups)
        b_g = conv[:, xw + g * N:xw + (g + 1) * N]
        c_g = conv[:, xw + gn + g * N:xw + gn + (g + 1) * N]
        x_h = xs[:, hd * P:(hd + 1) * P]
        x_col = jnp.sum(jnp.where(eye, jnp.broadcast_to(x_h, (P, P)), 0.0), axis=1, keepdims=True)
        h_new = decay[:, hd:hd + 1] * hprev_ref[0, hd] + (dt[:, hd:hd + 1] * x_col) * b_g
        hout_ref[0, hd] = h_new
        y_col = jnp.sum(h_new * c_g, axis=1, keepdims=True)
        y_row = jnp.sum(jnp.where(eye, jnp.broadcast_to(y_col, (P, P)), 0.0), axis=0, keepdims=True)
        y_parts.append(y_row + dsk_ref[0:1, hd:hd + 1] * x_h)
    y = jnp.concatenate(y_parts, axis=1)
    zg = z_ref[0]
    y = y * (zg * jax.nn.sigmoid(zg))
    gw = xw // n_groups
    nw = nw_ref[...]
    y_ref[0] = jnp.concatenate([_rms(y[:, g * gw:(g + 1) * gw], nw[:, g * gw:(g + 1) * gw])
                                for g in range(n_groups)], axis=1)


def _ssd_sample(proj3, conv_prev, ssm_prev, conv_w, conv_b, dt_bias, a_log, d_skip, norm_w, *, n_heads, n_groups):
    nb = proj3.shape[0]
    cdim = conv_w.shape[1]
    cz, cx = C_Z // COL_TILE, C_XBC // COL_TILE
    full = lambda shape: pl.BlockSpec(shape, lambda b: (0,) * len(shape))
    col = lambda cidx: pl.BlockSpec((1, 1, COL_TILE), lambda b: (b, 0, cidx))
    return pl.pallas_call(
        functools.partial(_ssd_sample_kernel, n_heads=n_heads, n_groups=n_groups),
        grid=(nb,),
        in_specs=[col(cz), col(cx), col(cx + 1),
                  pl.BlockSpec((1, CONV_TAPS - 1, cdim), lambda b: (b, 0, 0)),
                  pl.BlockSpec((1, n_heads, HEAD_DIM, SSM_STATE), lambda b: (b, 0, 0, 0)),
                  full((CONV_TAPS, cdim)), full((1, cdim)), full((1, LANES)), full((1, LANES)),
                  full((1, LANES)), full((1, W_BRANCH))],
        out_specs=[pl.BlockSpec((1, 1, W_BRANCH), lambda b: (b, 0, 0)),
                   pl.BlockSpec((1, n_heads, HEAD_DIM, SSM_STATE), lambda b: (b, 0, 0, 0)),
                   pl.BlockSpec((1, CONV_TAPS - 1, cdim), lambda b: (b, 0, 0))],
        out_shape=[jax.ShapeDtypeStruct((nb, 1, W_BRANCH), F32),
                   jax.ShapeDtypeStruct((nb, n_heads, HEAD_DIM, SSM_STATE), F32),
                   jax.ShapeDtypeStruct((nb, CONV_TAPS - 1, cdim), F32)],
        compiler_params=_cparams(("parallel",)),
    )(proj3, proj3, proj3, conv_prev, ssm_prev, conv_w, conv_b, dt_bias, a_log, d_skip, norm_w)


def _rope_tables(pos):
    half = HEAD_DIM // 2
    inv_freq = ROPE_THETA ** (-jnp.arange(half, dtype=F32) / half)
    ang = pos.astype(F32)[:, None] * inv_freq[None, :]
    cos, sin = jnp.cos(ang), jnp.sin(ang)
    reps = COL_TILE // HEAD_DIM
    cos_t = jnp.tile(jnp.concatenate([cos, cos], axis=1), (1, reps))
    sin_t = jnp.tile(jnp.concatenate([-sin, sin], axis=1), (1, reps))
    return cos_t, sin_t


def _rearrange_w_in(w):
    n_gate = 3 * D_MODEL
    split = C_DT - C_SBQ + 8
    body = w[:, :w.shape[1] - n_gate]
    pad = jnp.zeros((w.shape[0], C_DQ - C_SBQ - split), w.dtype)
    return jnp.concatenate([w[:, w.shape[1] - n_gate:], body[:, :split], pad, body[:, split:]], axis=1)


def _pad_lanes(v):
    v = v.reshape(1, -1)
    return jnp.pad(v, ((0, 0), (0, LANES - v.shape[1])))


def kernel(x_prompt, x_sample, cache_sb_k, cache_sb_v, cache_diff_k, cache_diff_v, state_ssm, state_conv,
           page_table, norm1_w, w_in, conv_w, conv_b, dt_bias, a_log, d_skip, ssm_norm_w,
           lam_q1, lam_k1, lam_q2, lam_k2, diff_norm_w, w_sb_out, w_ssm_out, w_diff_out, w_o,
           norm2_w, w_group, w_expert, w_gate, w_up, w_down, final_norm_w):
    bp, seq, d = x_prompt.shape
    bs = x_sample.shape[0]
    depth = w_in.shape[0]
    n_pages = page_table.shape[1]
    page = cache_sb_k.shape[2]
    sb_heads, diff_heads = cache_sb_k.shape[3], cache_diff_k.shape[3]
    n_heads = state_ssm.shape[2]
    cdim = state_conv.shape[3]
    n_groups = (cdim - n_heads * HEAD_DIM) // (2 * SSM_STATE)
    tp = bp * seq
    tq = min(256, seq)
    tm_proj = min(1024, seq)
    tm_merge = min(512, seq)
    tm_moe = min(1024, seq)
    pages_per_step = math.gcd(n_pages, 8)

    sb_k_t = jnp.transpose(cache_sb_k, (0, 1, 3, 4, 2))
    sb_v_t = jnp.transpose(cache_sb_v, (0, 1, 3, 4, 2))
    diff_k_r = cache_diff_k.reshape(depth, cache_diff_k.shape[1], page * diff_heads, 2 * HEAD_DIM)
    diff_v_r = cache_diff_v.reshape(depth, cache_diff_v.shape[1], page * diff_heads, 2 * HEAD_DIM)

    cos_p, sin_p = _rope_tables(jnp.arange(seq))
    cos_s, sin_s = _rope_tables(jnp.full((bs,), n_pages * page, jnp.int32))

    xp = x_prompt.reshape(tp, d)
    xs = x_sample.reshape(bs, d)
    outs = {k: [] for k in ("pk", "pv", "pdk", "pdv", "pssm", "pconv", "sk", "sv", "sdk", "sdv", "sssm", "sconv")}
    row = lambda v: v.reshape(1, -1)
    for l in range(depth):
        lam_init = 0.8 - 0.6 * math.exp(-0.3 * l)
        last = l == depth - 1
        w_in_f = _rearrange_w_in(w_in[l])
        lam_vecs = jnp.stack([lam_q1[l], lam_k1[l], lam_q2[l], lam_k2[l]])
        w_route = jnp.pad(jnp.concatenate([w_group[l], w_expert[l]], axis=1),
                          ((0, 0), (0, LANES - MOE_GROUPS - MOE_GROUPS * MOE_PER_GROUP)))
        ssd_w = (conv_w[l], row(conv_b[l]), _pad_lanes(dt_bias[l]), _pad_lanes(a_log[l]), _pad_lanes(d_skip[l]),
                 row(ssm_norm_w[l]))

        proj = _inproj(xp, row(norm1_w[l]), w_in_f.astype(BF16), cos_p, sin_p, precise=False, tm=tm_proj)
        o_sb = _sb_prompt(proj, batch=bp, seq=seq, tq=tq)
        o_diff = _diff_prompt(proj, lam_vecs, row(diff_norm_w[l]), batch=bp, seq=seq, tq=tq, lam_init=lam_init)
        y_ssm, ssm_p, conv_p = _ssd_prompt(proj, *ssd_w, batch=bp, seq=seq, n_heads=n_heads, n_groups=n_groups)
        xp = _merge(xp, proj, o_sb, y_ssm, o_diff, w_sb_out[l].astype(BF16), w_ssm_out[l].astype(BF16),
                    w_diff_out[l].astype(BF16), w_o[l].astype(BF16), precise=False, tm=tm_merge)
        xp = _moe(xp, row(norm2_w[l]), w_route, w_gate[l].astype(BF16), w_up[l].astype(BF16),
                  w_down[l].astype(BF16), row(final_norm_w), precise=False, final=last, tm=tm_moe)
        outs["pk"].append(proj[:, C_SBK:C_SBK + W_BRANCH].reshape(bp, seq, sb_heads, HEAD_DIM))
        outs["pv"].append(proj[:, C_SBV:C_SBV + W_BRANCH].reshape(bp, seq, sb_heads, HEAD_DIM))
        outs["pdk"].append(proj[:, C_DK:C_DK + W_BRANCH].reshape(bp, seq, diff_heads, 2 * HEAD_DIM))
        outs["pdv"].append(proj[:, C_DV:C_DV + W_BRANCH].reshape(bp, seq, diff_heads, 2 * HEAD_DIM))
        outs["pssm"].append(ssm_p)
        outs["pconv"].append(conv_p)

        proj_s = _inproj(xs, row(norm1_w[l]), w_in_f, cos_s, sin_s, precise=True, tm=bs)
        proj3 = proj_s.reshape(bs, 1, PROJ_W)
        sec = lambda c0: proj3[:, :, c0:c0 + W_BRANCH]
        o_sb_s = _sb_sample(page_table, sec(C_SBQ), sb_k_t, sb_v_t, l, pages_per_step=pages_per_step)
        o_diff_s = _diff_sample(page_table, sec(C_DQ), sec(C_DK), sec(C_DV), lam_vecs, row(diff_norm_w[l]),
                                diff_k_r, diff_v_r, l, pages_per_step=pages_per_step, n_heads=diff_heads,
                                lam_init=lam_init)
        y_ssm_s, ssm_s, conv_s = _ssd_sample(proj3, state_conv[l], state_ssm[l], *ssd_w,
                                             n_heads=n_heads, n_groups=n_groups)
        xs = _merge(xs, proj_s, o_sb_s.reshape(bs, -1), y_ssm_s.reshape(bs, -1), o_diff_s.reshape(bs, -1),
                    w_sb_out[l], w_ssm_out[l], w_diff_out[l], w_o[l], precise=True, tm=bs)
        xs = _moe(xs, row(norm2_w[l]), w_route, w_gate[l], w_up[l], w_down[l], row(final_norm_w),
                  precise=True, final=last, tm=bs)
        outs["sk"].append(proj_s[:, C_SBK:C_SBK + W_BRANCH].reshape(bs, 1, sb_heads, HEAD_DIM))
        outs["sv"].append(proj_s[:, C_SBV:C_SBV + W_BRANCH].reshape(bs, 1, sb_heads, HEAD_DIM))
        outs["sdk"].append(proj_s[:, C_DK:C_DK + W_BRANCH].reshape(bs, 1, diff_heads, 2 * HEAD_DIM))
        outs["sdv"].append(proj_s[:, C_DV:C_DV + W_BRANCH].reshape(bs, 1, diff_heads, 2 * HEAD_DIM))
        outs["sssm"].append(ssm_s)
        outs["sconv"].append(conv_s)

    st = {k: jnp.stack(v) for k, v in outs.items()}
    return (xp.reshape(bp, seq, d), xs.reshape(bs, 1, d),
            st["pk"], st["pv"], st["pdk"], st["pdv"], st["pssm"], st["pconv"],
            st["sk"], st["sv"], st["sdk"], st["sdv"], st["sssm"], st["sconv"])
```

```python
import functools
import math

import jax
import jax.numpy as jnp
from jax import lax
from jax.experimental import pallas as pl
from jax.experimental.pallas import tpu as pltpu

F32 = jnp.float32
BF16 = jnp.bfloat16

EPS = 1e-6
ROPE_THETA = 10000.0
LANES = 128
HEAD_DIM = 64
SSM_STATE = 64
SSM_CHUNK = 128
CONV_TAPS = 4
MOE_GROUPS = 4
MOE_PER_GROUP = 4
VMEM_LIMIT = 48 * 1024 * 1024
LOG2E = 1.4426950408889634
DIFF_KEY_CHUNK = 1024
SB_UNDERFLOW = 104.0

D_MODEL = 1024
W_BRANCH = 512
C_GATE = 0
C_SBQ = 3072
C_SBK = 3584
C_SBV = 4096
C_Z = 4608
C_XBC = 5120
C_DT = 5888
C_DQ = 6144
C_DK = 6656
C_DV = 7168
PROJ_W = 7680
COL_TILE = 512


def _cparams(sem):
    return pltpu.CompilerParams(dimension_semantics=sem, vmem_limit_bytes=VMEM_LIMIT)


def _hi_lo(a):
    hi = a.astype(BF16)
    lo = (a - hi.astype(F32)).astype(BF16)
    return hi, lo


_NN = (((1,), (0,)), ((), ()))
_NT = (((1,), (1,)), ((), ()))
_TN = (((0,), (0,)), ((), ()))


def _dg(a, b, dn):
    return lax.dot_general(a, b, dn, preferred_element_type=F32)


def _mm(a, b, precise, dn=_NN):
    if not precise:
        return _dg(a.astype(BF16), b.astype(BF16), dn)
    ah, al = _hi_lo(a.astype(F32))
    bh, bl = _hi_lo(b.astype(F32))
    return _dg(ah, bh, dn) + _dg(ah, bl, dn) + _dg(al, bh, dn)


def _mm_exact_rhs(a, b_exact, dn=_NN):
    ah, al = _hi_lo(a)
    return _dg(ah, b_exact, dn) + _dg(al, b_exact, dn)


def _mm_exact_lhs(a_exact, b, dn=_NN):
    bh, bl = _hi_lo(b)
    return _dg(a_exact, bh, dn) + _dg(a_exact, bl, dn)


def _softplus(z):
    return jnp.maximum(z, 0.0) + jnp.log(1.0 + jnp.exp(-jnp.abs(z)))


def _rms(x, w):
    return x * lax.rsqrt(jnp.mean(x * x, axis=-1, keepdims=True) + EPS) * w


def _inproj_kernel(x_ref, nw_ref, w_ref, cos_ref, sin_ref, o_ref, hh_ref, *maybe_hl, precise):
    j = pl.program_id(1)

    @pl.when(j == 0)
    def _():
        h = _rms(x_ref[...], nw_ref[...])
        hi = h.astype(BF16)
        hh_ref[...] = hi
        if precise:
            maybe_hl[0][...] = (h - hi.astype(F32)).astype(BF16)

    if precise:
        wh, wl = _hi_lo(w_ref[...])
        hh = hh_ref[...]
        acc = _dg(hh, wh, _NN) + _dg(hh, wl, _NN) + _dg(maybe_hl[0][...], wh, _NN)
    else:
        acc = _dg(hh_ref[...], w_ref[...], _NN)

    is_rope = jnp.logical_or(j == C_DQ // COL_TILE, j == C_DK // COL_TILE)

    @pl.when(is_rope)
    def _():
        lane = lax.broadcasted_iota(jnp.int32, acc.shape, 1)
        first = (lane % HEAD_DIM) < (HEAD_DIM // 2)
        partner = jnp.where(first, pltpu.roll(acc, COL_TILE - HEAD_DIM // 2, 1),
                            pltpu.roll(acc, HEAD_DIM // 2, 1))
        o_ref[...] = acc * cos_ref[...] + partner * sin_ref[...]

    @pl.when(jnp.logical_not(is_rope))
    def _():
        o_ref[...] = acc


def _inproj(x, norm_w, w, cos_t, sin_t, *, precise, tm):
    t = x.shape[0]
    n_pos_tiles = cos_t.shape[0] // tm
    scratch = [pltpu.VMEM((tm, D_MODEL), BF16)]
    if precise:
        scratch.append(pltpu.VMEM((tm, D_MODEL), BF16))
    return pl.pallas_call(
        functools.partial(_inproj_kernel, precise=precise),
        grid=(t // tm, PROJ_W // COL_TILE),
        in_specs=[
            pl.BlockSpec((tm, D_MODEL), lambda i, j: (i, 0)),
            pl.BlockSpec((1, D_MODEL), lambda i, j: (0, 0)),
            pl.BlockSpec((D_MODEL, COL_TILE), lambda i, j: (0, j)),
            pl.BlockSpec((tm, COL_TILE), lambda i, j: (i % n_pos_tiles, 0)),
            pl.BlockSpec((tm, COL_TILE), lambda i, j: (i % n_pos_tiles, 0)),
        ],
        out_specs=pl.BlockSpec((tm, COL_TILE), lambda i, j: (i, j)),
        out_shape=jax.ShapeDtypeStruct((t, PROJ_W), F32),
        scratch_shapes=scratch,
        compiler_params=_cparams(("parallel", "arbitrary")),
    )(x, norm_w, w, cos_t, sin_t)


def _sb_prompt_kernel(q_ref, k_ref, v_ref, o_ref, kb_ref, vb_ref, acc_ref, *, tq):
    qi = pl.program_id(2)
    lane_q = lax.broadcasted_iota(jnp.int32, (tq, LANES), 1)

    @pl.when(qi == 0)
    def _():
        lane = lax.broadcasted_iota(jnp.int32, (1, LANES), 1)
        v = v_ref[...]
        kb_ref[...] = k_ref[...].astype(BF16)
        vb_ref[0] = jnp.where(lane < HEAD_DIM, v, 0.0).astype(BF16)
        vb_ref[1] = jnp.where(lane >= HEAD_DIM, v, 0.0).astype(BF16)

    q = q_ref[...] * (1.0 / math.sqrt(HEAD_DIM))
    q_heads = (jnp.where(lane_q < HEAD_DIM, q, 0.0).astype(BF16),
               jnp.where(lane_q >= HEAD_DIM, q, 0.0).astype(BF16))
    row = lax.broadcasted_iota(jnp.int32, (tq, tq), 0)
    col = lax.broadcasted_iota(jnp.int32, (tq, tq), 1)
    later = jnp.where(row > col, 1.0, 0.0).astype(BF16)
    strictly_causal = col < row
    acc_ref[...] = jnp.zeros_like(acc_ref)

    def block(kb, carry, diagonal):
        start = pl.multiple_of(kb * tq, tq)
        k = kb_ref[pl.ds(start, tq), :]
        new_carry = []
        contrib = None
        for hd in range(2):
            z = _dg(q_heads[hd], k, _NT)
            sp = _softplus(z)
            if diagonal:
                sp = jnp.where(strictly_causal, sp, 0.0)
            after = _mm_exact_rhs(sp, later)
            total = after[:, 0:1] + sp[:, 0:1]
            w = jnp.exp(z - sp - after - carry[hd])
            if diagonal:
                w = jnp.where(strictly_causal, w, 0.0)
            pv = _dg(w.astype(BF16), vb_ref[hd, pl.ds(start, tq), :], _NN)
            contrib = pv if contrib is None else contrib + pv
            new_carry.append(carry[hd] + total)
        acc_ref[...] += contrib
        return tuple(new_carry)

    zero = jnp.zeros((tq, 1), F32)
    c0, c1 = block(qi, (zero, zero), True)

    def more(state):
        it, a, b = state
        return jnp.logical_and(it < qi, jnp.min(jnp.minimum(a, b)) < SB_UNDERFLOW)

    def step(state):
        it, a, b = state
        a, b = block(qi - 1 - it, (a, b), False)
        return it + 1, a, b

    lax.while_loop(more, step, (jnp.int32(0), c0, c1))
    o_ref[...] = acc_ref[...].astype(o_ref.dtype)


def _sb_prompt(proj, *, batch, seq, tq):
    nq = seq // tq
    t = batch * seq
    cq, ck, cv = C_SBQ // LANES, C_SBK // LANES, C_SBV // LANES
    return pl.pallas_call(
        functools.partial(_sb_prompt_kernel, tq=tq),
        grid=(batch, W_BRANCH // LANES, nq),
        in_specs=[
            pl.BlockSpec((tq, LANES), lambda b, hp, qi: (b * nq + qi, cq + hp)),
            pl.BlockSpec((seq, LANES), lambda b, hp, qi: (b, ck + hp)),
            pl.BlockSpec((seq, LANES), lambda b, hp, qi: (b, cv + hp)),
        ],
        out_specs=pl.BlockSpec((tq, LANES), lambda b, hp, qi: (b * nq + qi, hp)),
        out_shape=jax.ShapeDtypeStruct((t, W_BRANCH), BF16),
        scratch_shapes=[pltpu.VMEM((seq, LANES), BF16), pltpu.VMEM((2, seq, LANES), BF16),
                        pltpu.VMEM((tq, LANES), F32)],
        compiler_params=_cparams(("parallel", "parallel", "arbitrary")),
    )(proj, proj, proj)


def _lambda_value(lam_ref, lam_init):
    lv = lam_ref[...]
    s1 = jnp.sum(lv[0:1, :] * lv[1:2, :], axis=-1, keepdims=True)
    s2 = jnp.sum(lv[2:3, :] * lv[3:4, :], axis=-1, keepdims=True)
    return jnp.exp(s1) - jnp.exp(s2) + lam_init


def _diff_prompt_kernel(q_ref, k_ref, v_ref, lam_ref, nw_ref, o_ref, kb_ref, vb_ref, acc_ref, m_ref, l_ref,
                        *, tq, big, lam_init):
    qi = pl.program_id(2)

    @pl.when(qi == 0)
    def _():
        kb_ref[...] = k_ref[...].astype(BF16)
        vb_ref[...] = v_ref[...].astype(BF16)

    q = q_ref[...] * (LOG2E / math.sqrt(HEAD_DIM))
    lane_q = lax.broadcasted_iota(jnp.int32, (tq, LANES), 1)
    q_comp = (jnp.where(lane_q < HEAD_DIM, q, 0.0).astype(BF16),
              jnp.where(lane_q >= HEAD_DIM, q, 0.0).astype(BF16))
    row = lax.broadcasted_iota(jnp.int32, (tq, tq), 0)
    col = lax.broadcasted_iota(jnp.int32, (tq, tq), 1)
    causal = col <= row

    def block(start, size, diagonal, first):
        start = pl.multiple_of(start, tq)
        k = kb_ref[pl.ds(start, size), :]
        v = vb_ref[pl.ds(start, size), :]
        for c in range(2):
            s = _dg(q_comp[c], k, _NT)
            if diagonal:
                s = jnp.where(causal, s, -jnp.inf)
            blk_max = jnp.max(s, axis=-1, keepdims=True)
            if first:
                m_new = jnp.broadcast_to(blk_max, (tq, LANES))
            else:
                m_old = m_ref[c]
                m_new = jnp.maximum(m_old, blk_max)
                alpha = jnp.exp2(m_old - m_new)
            p = jnp.exp2(s - jnp.concatenate([m_new] * (size // LANES), axis=1))
            p_sum = jnp.sum(p, axis=-1, keepdims=True)
            pv = _dg(p.astype(BF16), v, _NN)
            if first:
                l_ref[c] = jnp.broadcast_to(p_sum, (tq, LANES))
                acc_ref[c] = pv
            else:
                l_ref[c] = alpha * l_ref[c] + p_sum
                acc_ref[c] = alpha * acc_ref[c] + pv
            m_ref[c] = m_new

    block(qi * tq, tq, True, True)
    n_big = (qi * tq) // big
    n_small = (qi * tq - n_big * big) // tq

    def big_body(it, _):
        block(it * big, big, False, False)
        return 0

    def small_body(it, _):
        block(n_big * big + it * tq, tq, False, False)
        return 0

    lax.fori_loop(0, n_big, big_body, 0)
    lax.fori_loop(0, n_small, small_body, 0)
    lam = _lambda_value(lam_ref, lam_init)
    o = acc_ref[0] / l_ref[0] - lam * (acc_ref[1] / l_ref[1])
    o_ref[...] = (_rms(o, nw_ref[...]) * (1.0 - lam_init)).astype(o_ref.dtype)


def _diff_prompt(proj, lam_vecs, diff_norm_w, *, batch, seq, tq, lam_init):
    nq = seq // tq
    t = batch * seq
    cq, ck, cv = C_DQ // LANES, C_DK // LANES, C_DV // LANES
    big = min(DIFF_KEY_CHUNK, seq)
    return pl.pallas_call(
        functools.partial(_diff_prompt_kernel, tq=tq, big=big, lam_init=lam_init),
        grid=(batch, W_BRANCH // LANES, nq),
        in_specs=[
            pl.BlockSpec((tq, LANES), lambda b, hh, qi: (b * nq + qi, cq + hh)),
            pl.BlockSpec((seq, LANES), lambda b, hh, qi: (b, ck + hh)),
            pl.BlockSpec((seq, LANES), lambda b, hh, qi: (b, cv + hh)),
            pl.BlockSpec((4, HEAD_DIM), lambda b, hh, qi: (0, 0)),
            pl.BlockSpec((1, LANES), lambda b, hh, qi: (0, 0)),
        ],
        out_specs=pl.BlockSpec((tq, LANES), lambda b, hh, qi: (b * nq + qi, hh)),
        out_shape=jax.ShapeDtypeStruct((t, W_BRANCH), BF16),
        scratch_shapes=[pltpu.VMEM((seq, LANES), BF16), pltpu.VMEM((seq, LANES), BF16),
                        pltpu.VMEM((2, tq, LANES), F32), pltpu.VMEM((2, tq, LANES), F32),
                        pltpu.VMEM((2, tq, LANES), F32)],
        compiler_params=_cparams(("parallel", "parallel", "arbitrary")),
    )(proj, proj, proj, lam_vecs, diff_norm_w)


def _ssd_prompt_kernel(z_ref, xa_ref, xb_ref, cw_ref, cb_ref, dtb_ref, alog_ref, dsk_ref, nw_ref,
                       y_ref, hout_ref, cout_ref, xpad_ref, h_ref, *, n_heads, n_groups):
    c = pl.program_id(1)
    nc = pl.num_programs(1)
    L = SSM_CHUNK
    P = HEAD_DIM
    N = SSM_STATE
    xw = n_heads * P
    gn = n_groups * N

    @pl.when(c == 0)
    def _():
        xpad_ref[0:8, :] = jnp.zeros((8, xpad_ref.shape[1]), F32)
        h_ref[...] = jnp.zeros_like(h_ref)

    xb = xb_ref[...]
    xbc = jnp.concatenate([xa_ref[...], xb[:, :2 * gn]], axis=1)
    xpad_ref[8:8 + L, :] = xbc
    conv = cb_ref[...]
    for tap in range(CONV_TAPS):
        conv = conv + xpad_ref[pl.ds(8 - (CONV_TAPS - 1) + tap, L), :] * cw_ref[tap:tap + 1, :]
    conv = conv * jax.nn.sigmoid(conv)
    xpad_ref[0:8, :] = xbc[L - 8:, :]

    @pl.when(c == nc - 1)
    def _():
        cout_ref[0] = xbc[L - (CONV_TAPS - 1):, :]

    xs = conv[:, :xw]
    bmat = conv[:, xw:xw + gn]
    cmat = conv[:, xw + gn:]

    lane = lax.broadcasted_iota(jnp.int32, (1, LANES), 1)
    head_lane = lane < n_heads
    dt = jnp.where(head_lane, _softplus(xb[:, 2 * gn:2 * gn + LANES] + dtb_ref[...]), 0.0)
    a = jnp.where(head_lane, -jnp.exp(alog_ref[...]), 0.0)
    dta = dt * a
    ri = lax.broadcasted_iota(jnp.int32, (L, L), 0)
    ci = lax.broadcasted_iota(jnp.int32, (L, L), 1)
    lower = jnp.where(ri >= ci, 1.0, 0.0).astype(BF16)
    dta_h, dta_l = _hi_lo(dta)
    dta_l2 = (dta - dta_h.astype(F32) - dta_l.astype(F32)).astype(BF16)
    cum = _dg(lower, dta_h, _NN) + _dg(lower, dta_l, _NN) + _dg(lower, dta_l2, _NN)
    cum_t = cum.T
    dt_t = dt.T
    causal = ri >= ci

    y_parts = []
    for g in range(n_groups):
        b_g = bmat[:, g * N:(g + 1) * N]
        c_g = cmat[:, g * N:(g + 1) * N]
        cb = _mm(c_g, b_g, False, _NT)
        for hh in range(n_heads // n_groups):
            hd = g * (n_heads // n_groups) + hh
            x_h = xs[:, hd * P:(hd + 1) * P]
            cum_col = cum[:, hd:hd + 1]
            cum_row = cum_t[hd:hd + 1, :]
            dt_col = dt[:, hd:hd + 1]
            dt_row = dt_t[hd:hd + 1, :]
            decay = jnp.exp(jnp.where(causal, cum_col - cum_row, -jnp.inf))
            y_intra = _mm(cb * decay * dt_row, x_h, False)
            h_prev = h_ref[hd]
            y_inter = _mm(c_g, h_prev, False, _NT) * jnp.exp(cum_col)
            cum_end = cum[L - 1:L, hd:hd + 1]
            w_end = jnp.exp(cum_end - cum_col) * dt_col
            s_chunk = _mm(x_h * w_end, b_g, False, _TN)
            h_ref[hd] = jnp.exp(cum_end) * h_prev + s_chunk
            y_parts.append(y_intra + y_inter + dsk_ref[0:1, hd:hd + 1] * x_h)
    y = jnp.concatenate(y_parts, axis=1)
    zg = z_ref[...]
    y = y * (zg * jax.nn.sigmoid(zg))
    gw = xw // n_groups
    nw = nw_ref[...]
    y = jnp.concatenate([_rms(y[:, g * gw:(g + 1) * gw], nw[:, g * gw:(g + 1) * gw])
                         for g in range(n_groups)], axis=1)
    y_ref[...] = y.astype(y_ref.dtype)

    @pl.when(c == nc - 1)
    def _():
        hout_ref[0] = h_ref[...]


def _ssd_prompt(proj, conv_w, conv_b, dt_bias, a_log, d_skip, norm_w, *, batch, seq, n_heads, n_groups):
    L = SSM_CHUNK
    nc = seq // L
    t = batch * seq
    cdim = conv_w.shape[1]
    cz, cx = C_Z // COL_TILE, C_XBC // COL_TILE
    full = lambda shape: pl.BlockSpec(shape, lambda b, c: (0,) * len(shape))
    return pl.pallas_call(
        functools.partial(_ssd_prompt_kernel, n_heads=n_heads, n_groups=n_groups),
        grid=(batch, nc),
        in_specs=[
            pl.BlockSpec((L, COL_TILE), lambda b, c: (b * nc + c, cz)),
            pl.BlockSpec((L, COL_TILE), lambda b, c: (b * nc + c, cx)),
            pl.BlockSpec((L, COL_TILE), lambda b, c: (b * nc + c, cx + 1)),
            full((CONV_TAPS, cdim)), full((1, cdim)), full((1, LANES)), full((1, LANES)),
            full((1, LANES)), full((1, W_BRANCH)),
        ],
        out_specs=[
            pl.BlockSpec((L, W_BRANCH), lambda b, c: (b * nc + c, 0)),
            pl.BlockSpec((1, n_heads, HEAD_DIM, SSM_STATE), lambda b, c: (b, 0, 0, 0)),
            pl.BlockSpec((1, CONV_TAPS - 1, cdim), lambda b, c: (b, 0, 0)),
        ],
        out_shape=[
            jax.ShapeDtypeStruct((t, W_BRANCH), BF16),
            jax.ShapeDtypeStruct((batch, n_heads, HEAD_DIM, SSM_STATE), F32),
            jax.ShapeDtypeStruct((batch, CONV_TAPS - 1, cdim), F32),
        ],
        scratch_shapes=[pltpu.VMEM((8 + L, cdim), F32), pltpu.VMEM((n_heads, HEAD_DIM, SSM_STATE), F32)],
        compiler_params=_cparams(("parallel", "arbitrary")),
    )(proj, proj, proj, conv_w, conv_b, dt_bias, a_log, d_skip, norm_w)


def _merge_kernel(x_ref, g0_ref, g1_ref, g2_ref, a_ref, b_ref, c_ref, wa_ref, wb_ref, wc_ref, wo_ref,
                  o_ref, *, precise):
    m = (jax.nn.sigmoid(g0_ref[...]) * _mm(a_ref[...], wa_ref[...], precise)
         + jax.nn.sigmoid(g1_ref[...]) * _mm(b_ref[...], wb_ref[...], precise)
         + jax.nn.sigmoid(g2_ref[...]) * _mm(c_ref[...], wc_ref[...], precise))
    o_ref[...] = x_ref[...] + _mm(m, wo_ref[...], precise)


def _merge(x, proj, o_sb, y_ssm, o_diff, w_sb, w_ssm, w_diff, w_o, *, precise, tm):
    t = x.shape[0]
    full = lambda shape: pl.BlockSpec(shape, lambda i: (0, 0))
    gate = lambda n: pl.BlockSpec((tm, D_MODEL), lambda i: (i, C_GATE // D_MODEL + n))
    branch = pl.BlockSpec((tm, W_BRANCH), lambda i: (i, 0))
    return pl.pallas_call(
        functools.partial(_merge_kernel, precise=precise),
        grid=(t // tm,),
        in_specs=[pl.BlockSpec((tm, D_MODEL), lambda i: (i, 0)), gate(0), gate(1), gate(2),
                  branch, branch, branch,
                  full((W_BRANCH, D_MODEL)), full((W_BRANCH, D_MODEL)), full((W_BRANCH, D_MODEL)),
                  full((D_MODEL, D_MODEL))],
        out_specs=pl.BlockSpec((tm, D_MODEL), lambda i: (i, 0)),
        out_shape=jax.ShapeDtypeStruct((t, D_MODEL), F32),
        compiler_params=_cparams(("parallel",)),
    )(x, proj, proj, proj, o_sb, y_ssm, o_diff, w_sb, w_ssm, w_diff, w_o)


def _route(logits):
    lane = lax.broadcasted_iota(jnp.int32, logits.shape, 1)
    neg = -jnp.inf
    big = jnp.int32(1 << 20)
    g = jnp.where(lane < MOE_GROUPS, logits, neg)
    g_max = jnp.max(g, axis=-1, keepdims=True)
    g_sel = jnp.min(jnp.where(g == g_max, lane, big), axis=-1, keepdims=True)
    g_w = 1.0 / jnp.sum(jnp.exp(g - g_max), axis=-1, keepdims=True)
    lo = MOE_GROUPS + g_sel * MOE_PER_GROUP
    e = jnp.where((lane >= lo) & (lane < lo + MOE_PER_GROUP), logits, neg)
    v1 = jnp.max(e, axis=-1, keepdims=True)
    i1 = jnp.min(jnp.where(e == v1, lane, big), axis=-1, keepdims=True)
    e2 = jnp.where(lane == i1, neg, e)
    v2 = jnp.max(e2, axis=-1, keepdims=True)
    i2 = jnp.min(jnp.where(e2 == v2, lane, big), axis=-1, keepdims=True)
    r = jnp.exp(v2 - v1)
    w1 = g_w / (1.0 + r)
    w2 = g_w * r / (1.0 + r)
    return jnp.where(lane == i1, w1, 0.0) + jnp.where(lane == i2, w2, 0.0)


def _moe_kernel(x_ref, nw_ref, wr_ref, wg_ref, wu_ref, wd_ref, fw_ref, o_ref,
                hh_ref, hl_ref, comb_ref, acc_ref, *, precise, final):
    e = pl.program_id(1)
    ne = pl.num_programs(1)

    @pl.when(e == 0)
    def _():
        h = _rms(x_ref[...], nw_ref[...])
        hi = h.astype(BF16)
        lo = (h - hi.astype(F32)).astype(BF16)
        hh_ref[...] = hi
        hl_ref[...] = lo
        wrh, wrl = _hi_lo(wr_ref[...])
        logits = _dg(hi, wrh, _NN) + _dg(hi, wrl, _NN) + _dg(lo, wrh, _NN)
        comb_ref[...] = _route(logits)
        acc_ref[...] = jnp.zeros_like(acc_ref)

    hh = hh_ref[...]
    if precise:
        hl = hl_ref[...]
        gh, gl = _hi_lo(wg_ref[0])
        uh, ul = _hi_lo(wu_ref[0])
        gate = _dg(hh, gh, _NN) + _dg(hh, gl, _NN) + _dg(hl, gh, _NN)
        up = _dg(hh, uh, _NN) + _dg(hh, ul, _NN) + _dg(hl, uh, _NN)
    else:
        gate = _dg(hh, wg_ref[0], _NN)
        up = _dg(hh, wu_ref[0], _NN)
    lane = lax.broadcasted_iota(jnp.int32, comb_ref.shape, 1)
    cw = jnp.sum(jnp.where(lane == e + MOE_GROUPS, comb_ref[...], 0.0), axis=-1, keepdims=True)
    he = gate * jax.nn.sigmoid(gate) * up * cw
    acc_ref[...] += _mm(he, wd_ref[0], precise)

    @pl.when(e == ne - 1)
    def _():
        y = x_ref[...] + acc_ref[...]
        if final:
            y = _rms(y, fw_ref[...])
        o_ref[...] = y


def _moe(x, norm_w, w_route, w_gate, w_up, w_down, final_w, *, precise, final, tm):
    t = x.shape[0]
    n_exp, _, ff = w_gate.shape
    return pl.pallas_call(
        functools.partial(_moe_kernel, precise=precise, final=final),
        grid=(t // tm, n_exp),
        in_specs=[
            pl.BlockSpec((tm, D_MODEL), lambda i, e: (i, 0)),
            pl.BlockSpec((1, D_MODEL), lambda i, e: (0, 0)),
            pl.BlockSpec((D_MODEL, LANES), lambda i, e: (0, 0)),
            pl.BlockSpec((1, D_MODEL, ff), lambda i, e: (e, 0, 0)),
            pl.BlockSpec((1, D_MODEL, ff), lambda i, e: (e, 0, 0)),
            pl.BlockSpec((1, ff, D_MODEL), lambda i, e: (e, 0, 0)),
            pl.BlockSpec((1, D_MODEL), lambda i, e: (0, 0)),
        ],
        out_specs=pl.BlockSpec((tm, D_MODEL), lambda i, e: (i, 0)),
        out_shape=jax.ShapeDtypeStruct((t, D_MODEL), F32),
        scratch_shapes=[pltpu.VMEM((tm, D_MODEL), BF16), pltpu.VMEM((tm, D_MODEL), BF16),
                        pltpu.VMEM((tm, LANES), F32), pltpu.VMEM((tm, D_MODEL), F32)],
        compiler_params=_cparams(("parallel", "arbitrary")),
    )(x, norm_w, w_route, w_gate, w_up, w_down, final_w)


def _row_to_col(row, eye):
    return jnp.sum(jnp.where(eye, jnp.broadcast_to(row, eye.shape), 0.0), axis=1, keepdims=True)


def _col_to_row(col, eye):
    return jnp.sum(jnp.where(eye, jnp.broadcast_to(col, eye.shape), 0.0), axis=0, keepdims=True)


def _sb_sample_kernel(*refs, pages_per_step, n_heads, resume):
    if resume:
        pt_ref, more_ref, q_ref, prev_o_ref, prev_c_ref = refs[:5]
        refs = refs[5:]
    else:
        pt_ref, q_ref = refs[:2]
        refs = refs[2:]
    k_refs = refs[:pages_per_step]
    v_refs = refs[pages_per_step:2 * pages_per_step]
    if resume:
        o_ref, qb_ref, acc_ref, c_ref = refs[2 * pages_per_step:]
    else:
        o_ref, c_out_ref, more_out_ref, qb_ref, acc_ref, c_ref = refs[2 * pages_per_step:]
    b = pl.program_id(0)
    j = pl.program_id(1)
    nj = pl.num_programs(1)
    P = HEAD_DIM
    page = qb_ref.shape[-1]
    eye = lax.broadcasted_iota(jnp.int32, (P, P), 0) == lax.broadcasted_iota(jnp.int32, (P, P), 1)

    @pl.when(j == 0)
    def _():
        q = q_ref[0] * (1.0 / math.sqrt(HEAD_DIM))
        for hd in range(n_heads):
            qb_ref[hd] = jnp.broadcast_to(_row_to_col(q[:, hd * P:(hd + 1) * P], eye), (P, page))
        acc_ref[...] = jnp.zeros_like(acc_ref)
        c_ref[...] = prev_c_ref[0] if resume else jnp.zeros_like(c_ref)

    def walk():
        ri = lax.broadcasted_iota(jnp.int32, (page, page), 0)
        ci = lax.broadcasted_iota(jnp.int32, (page, page), 1)
        later = jnp.where(ri > ci, 1.0, 0.0).astype(BF16)
        carry = c_ref[:, 0:1]
        for p in range(pages_per_step):
            z = jnp.concatenate([jnp.sum(k_refs[p][0, 0, hd] * qb_ref[hd], axis=0, keepdims=True)
                                 for hd in range(n_heads)], axis=0)
            sp = _softplus(z)
            sph, spl = _hi_lo(sp)
            spl2 = (sp - sph.astype(F32) - spl.astype(F32)).astype(BF16)
            after = _dg(sph, later, _NN) + _dg(spl, later, _NN) + _dg(spl2, later, _NN)
            w = jnp.exp(z - sp - after - carry)
            for hd in range(n_heads):
                acc_ref[hd] += v_refs[p][0, 0, hd] * w[hd:hd + 1, :]
            carry = carry + after[:, 0:1] + sp[:, 0:1]
        c_ref[...] = jnp.broadcast_to(carry, c_ref.shape)

    if resume:
        pl.when(more_ref[b] != 0)(walk)
    else:
        walk()

    @pl.when(j == nj - 1)
    def _():
        o = jnp.concatenate(
            [_col_to_row(jnp.sum(acc_ref[hd], axis=1, keepdims=True), eye) for hd in range(n_heads)], axis=1)
        if resume:
            o_ref[0] = o + prev_o_ref[0]
        else:
            o_ref[0] = o
            carry = c_ref[...]
            c_out_ref[0] = carry
            more_out_ref[0] = jnp.broadcast_to(jnp.where(jnp.min(carry) < SB_UNDERFLOW, 1, 0).astype(jnp.int32),
                                               (1, LANES))


def _sb_sample_walk(page_table, q, cache_k_t, cache_v_t, layer, newest, n_steps, resume_args, *, pages_per_step):
    nb, n_pages = page_table.shape
    _, _, n_heads, hdim, page = cache_k_t.shape
    width = n_heads * hdim
    pt_flat = page_table.reshape(-1)
    resume = resume_args is not None

    def page_spec(p):
        def index(b, j, pt, *more):
            phys = pt[b * n_pages + newest - j * pages_per_step - p]
            if resume:
                phys = jnp.where(more[0][b] != 0, phys, 0)
            return (layer, phys, 0, 0, 0)
        return pl.BlockSpec((1, 1, n_heads, hdim, page), index)

    specs = [page_spec(p) for p in range(pages_per_step)]
    row = pl.BlockSpec((1, 1, width), lambda b, j, *_: (b, 0, 0))
    carry_spec = pl.BlockSpec((1, n_heads, LANES), lambda b, j, *_: (b, 0, 0))
    flag_spec = pl.BlockSpec((1, 1, LANES), lambda b, j, *_: (b, 0, 0))
    scratch = [pltpu.VMEM((n_heads, hdim, page), F32), pltpu.VMEM((n_heads, hdim, page), F32),
               pltpu.VMEM((n_heads, LANES), F32)]
    kernel_fn = functools.partial(_sb_sample_kernel, pages_per_step=pages_per_step, n_heads=n_heads, resume=resume)
    pages = [cache_k_t] * pages_per_step + [cache_v_t] * pages_per_step
    if resume:
        more, prev_o, prev_c = resume_args
        grid_spec = pltpu.PrefetchScalarGridSpec(
            num_scalar_prefetch=2, grid=(nb, n_steps),
            in_specs=[row, row, carry_spec] + specs + specs, out_specs=row, scratch_shapes=scratch)
        return pl.pallas_call(
            kernel_fn, grid_spec=grid_spec, out_shape=jax.ShapeDtypeStruct((nb, 1, width), F32),
            compiler_params=_cparams(("parallel", "arbitrary")),
        )(pt_flat, more, q, prev_o, prev_c, *pages)
    grid_spec = pltpu.PrefetchScalarGridSpec(
        num_scalar_prefetch=1, grid=(nb, n_steps),
        in_specs=[row] + specs + specs, out_specs=[row, carry_spec, flag_spec], scratch_shapes=scratch)
    return pl.pallas_call(
        kernel_fn, grid_spec=grid_spec,
        out_shape=[jax.ShapeDtypeStruct((nb, 1, width), F32), jax.ShapeDtypeStruct((nb, n_heads, LANES), F32),
                   jax.ShapeDtypeStruct((nb, 1, LANES), jnp.int32)],
        compiler_params=_cparams(("parallel", "arbitrary")),
    )(pt_flat, q, *pages)


def _sb_sample(page_table, q, cache_k_t, cache_v_t, layer, *, pages_per_step):
    n_pages = page_table.shape[1]
    o, carry, more = _sb_sample_walk(page_table, q, cache_k_t, cache_v_t, layer, n_pages - 1, 1, None,
                                     pages_per_step=pages_per_step)
    n_steps = n_pages // pages_per_step - 1
    if n_steps == 0:
        return o
    return _sb_sample_walk(page_table, q, cache_k_t, cache_v_t, layer, n_pages - 1 - pages_per_step, n_steps,
                           (more[:, 0, 0], o, carry), pages_per_step=pages_per_step)


def _diff_sample_kernel(pt_ref, q_ref, kn_ref, vn_ref, lam_ref, nw_ref, *refs, pages_per_step, n_heads, lam_init):
    k_refs = refs[:pages_per_step]
    v_refs = refs[pages_per_step:2 * pages_per_step]
    o_ref, q8_ref, acc_ref, m_ref, l_ref = refs[2 * pages_per_step:]
    j = pl.program_id(1)
    nj = pl.num_programs(1)
    rows = k_refs[0].shape[2]
    comp_of_lane = lax.broadcasted_iota(jnp.int32, (LANES, LANES), 0) // HEAD_DIM
    sel = [jnp.where(comp_of_lane == c, 1.0, 0.0).astype(BF16) for c in range(2)]
    row8 = lax.broadcasted_iota(jnp.int32, (8, LANES), 0)

    def per_sublane(vec):
        return jnp.concatenate([vec[:, (i % n_heads) * LANES:(i % n_heads + 1) * LANES] for i in range(8)], axis=0)

    @pl.when(j == 0)
    def _():
        q8 = per_sublane(q_ref[0]) * (1.0 / math.sqrt(HEAD_DIM))
        q8_ref[...] = q8
        prod = per_sublane(kn_ref[0]) * q8
        v_new = jnp.where(row8 < n_heads, per_sublane(vn_ref[0]), 0.0)
        for c in range(2):
            m_ref[c] = _mm_exact_rhs(prod, sel[c])
            l_ref[c] = jnp.where(row8 < n_heads, 1.0, 0.0)
            acc_ref[c] = v_new

    q8 = q8_ref[...]
    for p in range(pages_per_step):
        k3 = k_refs[p][0, 0].reshape(rows // 8, 8, LANES)
        v3 = v_refs[p][0, 0].reshape(rows // 8, 8, LANES)
        prod = (k3 * q8[None]).reshape(rows, LANES)
        for c in range(2):
            s = _mm_exact_rhs(prod, sel[c]).reshape(rows // 8, 8, LANES)
            blk_max = jnp.max(s, axis=0)
            blk_max = jnp.maximum(blk_max, pltpu.roll(blk_max, n_heads, 0))
            m_old = m_ref[c]
            m_new = jnp.maximum(m_old, blk_max)
            alpha = jnp.exp(m_old - m_new)
            pr = jnp.exp(s - m_new[None])
            l_ref[c] = alpha * l_ref[c] + jnp.sum(pr, axis=0)
            acc_ref[c] = alpha * acc_ref[c] + jnp.sum(pr * v3, axis=0)
            m_ref[c] = m_new

    @pl.when(j == nj - 1)
    def _():
        lam = _lambda_value(lam_ref, lam_init)
        outs = []
        for c in range(2):
            l = l_ref[c]
            a = acc_ref[c]
            outs.append((a + pltpu.roll(a, n_heads, 0)) / (l + pltpu.roll(l, n_heads, 0)))
        o = _rms(outs[0] - lam * outs[1], nw_ref[...]) * (1.0 - lam_init)
        o_ref[0] = jnp.concatenate([o[hd:hd + 1, :] for hd in range(n_heads)], axis=1)


def _diff_sample(page_table, q, k_new, v_new, lam_vecs, diff_norm_w, cache_k_r, cache_v_r, layer, *,
                 pages_per_step, n_heads, lam_init):
    nb, n_pages = page_table.shape
    rows = cache_k_r.shape[2]
    width = n_heads * LANES
    nj = n_pages // pages_per_step
    pt_flat = page_table.reshape(-1)

    def page_spec(p):
        def index(b, j, pt):
            return (layer, pt[b * n_pages + j * pages_per_step + p], 0, 0)
        return pl.BlockSpec((1, 1, rows, LANES), index)

    specs = [page_spec(p) for p in range(pages_per_step)]
    row = pl.BlockSpec((1, 1, width), lambda b, j, pt: (b, 0, 0))
    grid_spec = pltpu.PrefetchScalarGridSpec(
        num_scalar_prefetch=1,
        grid=(nb, nj),
        in_specs=[row, row, row,
                  pl.BlockSpec((4, HEAD_DIM), lambda b, j, pt: (0, 0)),
                  pl.BlockSpec((1, LANES), lambda b, j, pt: (0, 0))] + specs + specs,
        out_specs=row,
        scratch_shapes=[pltpu.VMEM((8, LANES), F32), pltpu.VMEM((2, 8, LANES), F32),
                        pltpu.VMEM((2, 8, LANES), F32), pltpu.VMEM((2, 8, LANES), F32)],
    )
    return pl.pallas_call(
        functools.partial(_diff_sample_kernel, pages_per_step=pages_per_step, n_heads=n_heads, lam_init=lam_init),
        grid_spec=grid_spec,
        out_shape=jax.ShapeDtypeStruct((nb, 1, width), F32),
        compiler_params=_cparams(("parallel", "arbitrary")),
    )(pt_flat, q, k_new, v_new, lam_vecs, diff_norm_w,
      *([cache_k_r] * pages_per_step), *([cache_v_r] * pages_per_step))


def _ssd_sample_kernel(z_ref, xa_ref, xb_ref, cprev_ref, hprev_ref, cw_ref, cb_ref, dtb_ref, alog_ref, dsk_ref,
                       nw_ref, y_ref, hout_ref, cout_ref, *, n_heads, n_groups):
    P = HEAD_DIM
    N = SSM_STATE
    xw = n_heads * P
    gn = n_groups * N
    xb = xb_ref[0]
    xbc = jnp.concatenate([xa_ref[0], xb[:, :2 * gn]], axis=1)
    prev = cprev_ref[0]
    conv = cb_ref[...] + xbc * cw_ref[CONV_TAPS - 1:CONV_TAPS, :]
    for tap in range(CONV_TAPS - 1):
        conv = conv + prev[tap:tap + 1, :] * cw_ref[tap:tap + 1, :]
    conv = conv * jax.nn.sigmoid(conv)
    cout_ref[0, 0:CONV_TAPS - 2, :] = prev[1:, :]
    cout_ref[0, CONV_TAPS - 2:CONV_TAPS - 1, :] = xbc
    xs = conv[:, :xw]
    dt = _softplus(xb[:, 2 * gn:2 * gn + LANES] + dtb_ref[...])
    a = -jnp.exp(alog_ref[...])
    decay = jnp.exp(dt * a)
    eye = lax.broadcasted_iota(jnp.int32, (P, P), 0) == lax.broadcasted_iota(jnp.int32, (P, P), 1)
    y_parts = []
    for hd in range(n_heads):
        g = hd // (n_heads // n_groups)
        b_g = conv[:, xw + g * N:xw + (g + 1) * N]
        c_g = conv[:, xw + gn + g * N:xw + gn + (g + 1) * N]
        x_h = xs[:, hd * P:(hd + 1) * P]
        x_col = jnp.sum(jnp.where(eye, jnp.broadcast_to(x_h, (P, P)), 0.0), axis=1, keepdims=True)
        h_new = decay[:, hd:hd + 1] * hprev_ref[0, hd] + (dt[:, hd:hd + 1] * x_col) * b_g
        hout_ref[0, hd] = h_new
        y_col = jnp.sum(h_new * c_g, axis=1, keepdims=True)
        y_row = jnp.sum(jnp.where(eye, jnp.broadcast_to(y_col, (P, P)), 0.0), axis=0, keepdims=True)
        y_parts.append(y_row + dsk_ref[0:1, hd:hd + 1] * x_h)
    y = jnp.concatenate(y_parts, axis=1)
    zg = z_ref[0]
    y = y * (zg * jax.nn.sigmoid(zg))
    gw = xw // n_groups
    nw = nw_ref[...]
    y_ref[0] = jnp.concatenate([_rms(y[:, g * gw:(g + 1) * gw], nw[:, g * gw:(g + 1) * gw])
                                for g in range(n_groups)], axis=1)


def _ssd_sample(proj3, conv_prev, ssm_prev, conv_w, conv_b, dt_bias, a_log, d_skip, norm_w, *, n_heads, n_groups):
    nb = proj3.shape[0]
    cdim = conv_w.shape[1]
    cz, cx = C_Z // COL_TILE, C_XBC // COL_TILE
    full = lambda shape: pl.BlockSpec(shape, lambda b: (0,) * len(shape))
    col = lambda cidx: pl.BlockSpec((1, 1, COL_TILE), lambda b: (b, 0, cidx))
    return pl.pallas_call(
        functools.partial(_ssd_sample_kernel, n_heads=n_heads, n_groups=n_groups),
        grid=(nb,),
        in_specs=[col(cz), col(cx), col(cx + 1),
                  pl.BlockSpec((1, CONV_TAPS - 1, cdim), lambda b: (b, 0, 0)),
                  pl.BlockSpec((1, n_heads, HEAD_DIM, SSM_STATE), lambda b: (b, 0, 0, 0)),
                  full((CONV_TAPS, cdim)), full((1, cdim)), full((1, LANES)), full((1, LANES)),
                  full((1, LANES)), full((1, W_BRANCH))],
        out_specs=[pl.BlockSpec((1, 1, W_BRANCH), lambda b: (b, 0, 0)),
                   pl.BlockSpec((1, n_heads, HEAD_DIM, SSM_STATE), lambda b: (b, 0, 0, 0)),
                   pl.BlockSpec((1, CONV_TAPS - 1, cdim), lambda b: (b, 0, 0))],
        out_shape=[jax.ShapeDtypeStruct((nb, 1, W_BRANCH), F32),
                   jax.ShapeDtypeStruct((nb, n_heads, HEAD_DIM, SSM_STATE), F32),
                   jax.ShapeDtypeStruct((nb, CONV_TAPS - 1, cdim), F32)],
        compiler_params=_cparams(("parallel",)),
    )(proj3, proj3, proj3, conv_prev, ssm_prev, conv_w, conv_b, dt_bias, a_log, d_skip, norm_w)


def _rope_tables(pos):
    half = HEAD_DIM // 2
    inv_freq = ROPE_THETA ** (-jnp.arange(half, dtype=F32) / half)
    ang = pos.astype(F32)[:, None] * inv_freq[None, :]
    cos, sin = jnp.cos(ang), jnp.sin(ang)
    reps = COL_TILE // HEAD_DIM
    cos_t = jnp.tile(jnp.concatenate([cos, cos], axis=1), (1, reps))
    sin_t = jnp.tile(jnp.concatenate([-sin, sin], axis=1), (1, reps))
    return cos_t, sin_t


def _rearrange_w_in(w):
    n_gate = 3 * D_MODEL
    split = C_DT - C_SBQ + 8
    body = w[:, :w.shape[1] - n_gate]
    pad = jnp.zeros((w.shape[0], C_DQ - C_SBQ - split), w.dtype)
    return jnp.concatenate([w[:, w.shape[1] - n_gate:], body[:, :split], pad, body[:, split:]], axis=1)


def _pad_lanes(v):
    v = v.reshape(1, -1)
    return jnp.pad(v, ((0, 0), (0, LANES - v.shape[1])))


def kernel(x_prompt, x_sample, cache_sb_k, cache_sb_v, cache_diff_k, cache_diff_v, state_ssm, state_conv,
           page_table, norm1_w, w_in, conv_w, conv_b, dt_bias, a_log, d_skip, ssm_norm_w,
           lam_q1, lam_k1, lam_q2, lam_k2, diff_norm_w, w_sb_out, w_ssm_out, w_diff_out, w_o,
           norm2_w, w_group, w_expert, w_gate, w_up, w_down, final_norm_w):
    bp, seq, d = x_prompt.shape
    bs = x_sample.shape[0]
    depth = w_in.shape[0]
    n_pages = page_table.shape[1]
    page = cache_sb_k.shape[2]
    sb_heads, diff_heads = cache_sb_k.shape[3], cache_diff_k.shape[3]
    n_heads = state_ssm.shape[2]
    cdim = state_conv.shape[3]
    n_groups = (cdim - n_heads * HEAD_DIM) // (2 * SSM_STATE)
    tp = bp * seq
    tq = min(256, seq)
    tm_proj = min(1024, seq)
    tm_merge = min(512, seq)
    tm_moe = min(1024, seq)
    pages_per_step = math.gcd(n_pages, 8)

    sb_k_t = jnp.transpose(cache_sb_k, (0, 1, 3, 4, 2))
    sb_v_t = jnp.transpose(cache_sb_v, (0, 1, 3, 4, 2))
    diff_k_r = cache_diff_k.reshape(depth, cache_diff_k.shape[1], page * diff_heads, 2 * HEAD_DIM)
    diff_v_r = cache_diff_v.reshape(depth, cache_diff_v.shape[1], page * diff_heads, 2 * HEAD_DIM)

    cos_p, sin_p = _rope_tables(jnp.arange(seq))
    cos_s, sin_s = _rope_tables(jnp.full((bs,), n_pages * page, jnp.int32))

    xp = x_prompt.reshape(tp, d)
    xs = x_sample.reshape(bs, d)
    outs = {k: [] for k in ("pk", "pv", "pdk", "pdv", "pssm", "pconv", "sk", "sv", "sdk", "sdv", "sssm", "sconv")}
    row = lambda v: v.reshape(1, -1)
    for l in range(depth):
        lam_init = 0.8 - 0.6 * math.exp(-0.3 * l)
        last = l == depth - 1
        w_in_f = _rearrange_w_in(w_in[l])
        lam_vecs = jnp.stack([lam_q1[l], lam_k1[l], lam_q2[l], lam_k2[l]])
        w_route = jnp.pad(jnp.concatenate([w_group[l], w_expert[l]], axis=1),
                          ((0, 0), (0, LANES - MOE_GROUPS - MOE_GROUPS * MOE_PER_GROUP)))
        ssd_w = (conv_w[l], row(conv_b[l]), _pad_lanes(dt_bias[l]), _pad_lanes(a_log[l]), _pad_lanes(d_skip[l]),
                 row(ssm_norm_w[l]))

        proj = _inproj(xp, row(norm1_w[l]), w_in_f.astype(BF16), cos_p, sin_p, precise=False, tm=tm_proj)
        o_sb = _sb_prompt(proj, batch=bp, seq=seq, tq=tq)
        o_diff = _diff_prompt(proj, lam_vecs, row(diff_norm_w[l]), batch=bp, seq=seq, tq=tq, lam_init=lam_init)
        y_ssm, ssm_p, conv_p = _ssd_prompt(proj, *ssd_w, batch=bp, seq=seq, n_heads=n_heads, n_groups=n_groups)
        xp = _merge(xp, proj, o_sb, y_ssm, o_diff, w_sb_out[l].astype(BF16), w_ssm_out[l].astype(BF16),
                    w_diff_out[l].astype(BF16), w_o[l].astype(BF16), precise=False, tm=tm_merge)
        xp = _moe(xp, row(norm2_w[l]), w_route, w_gate[l].astype(BF16), w_up[l].astype(BF16),
                  w_down[l].astype(BF16), row(final_norm_w), precise=False, final=last, tm=tm_moe)
        outs["pk"].append(proj[:, C_SBK:C_SBK + W_BRANCH].reshape(bp, seq, sb_heads, HEAD_DIM))
        outs["pv"].append(proj[:, C_SBV:C_SBV + W_BRANCH].reshape(bp, seq, sb_heads, HEAD_DIM))
        outs["pdk"].append(proj[:, C_DK:C_DK + W_BRANCH].reshape(bp, seq, diff_heads, 2 * HEAD_DIM))
        outs["pdv"].append(proj[:, C_DV:C_DV + W_BRANCH].reshape(bp, seq, diff_heads, 2 * HEAD_DIM))
        outs["pssm"].append(ssm_p)
        outs["pconv"].append(conv_p)

        proj_s = _inproj(xs, row(norm1_w[l]), w_in_f, cos_s, sin_s, precise=True, tm=bs)
        proj3 = proj_s.reshape(bs, 1, PROJ_W)
        sec = lambda c0: proj3[:, :, c0:c0 + W_BRANCH]
        o_sb_s = _sb_sample(page_table, sec(C_SBQ), sb_k_t, sb_v_t, l, pages_per_step=pages_per_step)
        o_diff_s = _diff_sample(page_table, sec(C_DQ), sec(C_DK), sec(C_DV), lam_vecs, row(diff_norm_w[l]),
                                diff_k_r, diff_v_r, l, pages_per_step=pages_per_step, n_heads=diff_heads,
                                lam_init=lam_init)
        y_ssm_s, ssm_s, conv_s = _ssd_sample(proj3, state_conv[l], state_ssm[l], *ssd_w,
                                             n_heads=n_heads, n_groups=n_groups)
        xs = _merge(xs, proj_s, o_sb_s.reshape(bs, -1), y_ssm_s.reshape(bs, -1), o_diff_s.reshape(bs, -1),
                    w_sb_out[l], w_ssm_out[l], w_diff_out[l], w_o[l], precise=True, tm=bs)
        xs = _moe(xs, row(norm2_w[l]), w_route, w_gate[l], w_up[l], w_down[l], row(final_norm_w),
                  precise=True, final=last, tm=bs)
        outs["sk"].append(proj_s[:, C_SBK:C_SBK + W_BRANCH].reshape(bs, 1, sb_heads, HEAD_DIM))
        outs["sv"].append(proj_s[:, C_SBV:C_SBV + W_BRANCH].reshape(bs, 1, sb_heads, HEAD_DIM))
        outs["sdk"].append(proj_s[:, C_DK:C_DK + W_BRANCH].reshape(bs, 1, diff_heads, 2 * HEAD_DIM))
        outs["sdv"].append(proj_s[:, C_DV:C_DV + W_BRANCH].reshape(bs, 1, diff_heads, 2 * HEAD_DIM))
        outs["sssm"].append(ssm_s)
        outs["sconv"].append(conv_s)

    st = {k: jnp.stack(v) for k, v in outs.items()}
    return (xp.reshape(bp, seq, d), xs.reshape(bs, 1, d),
            st["pk"], st["pv"], st["pdk"], st["pdv"], st["pssm"], st["pconv"],
            st["sk"], st["sv"], st["sdk"], st["sdv"], st["sssm"], st["sconv"])
```

```python
import functools
import math

import jax
import jax.numpy as jnp
from jax import lax
from jax.experimental import pallas as pl
from jax.experimental.pallas import tpu as pltpu

F32 = jnp.float32
BF16 = jnp.bfloat16

EPS = 1e-6
ROPE_THETA = 10000.0
LANES = 128
HEAD_DIM = 64
SSM_STATE = 64
SSM_CHUNK = 128
CONV_TAPS = 4
MOE_GROUPS = 4
MOE_PER_GROUP = 4
VMEM_LIMIT = 48 * 1024 * 1024
LOG2E = 1.4426950408889634
DIFF_KEY_CHUNK = 1024
SB_UNDERFLOW = 104.0

D_MODEL = 1024
W_BRANCH = 512
C_GATE = 0
C_SBQ = 3072
C_SBK = 3584
C_SBV = 4096
C_Z = 4608
C_XBC = 5120
C_DT = 5888
C_DQ = 6144
C_DK = 6656
C_DV = 7168
PROJ_W = 7680
COL_TILE = 512


def _cparams(sem):
    return pltpu.CompilerParams(dimension_semantics=sem, vmem_limit_bytes=VMEM_LIMIT)


def _hi_lo(a):
    hi = a.astype(BF16)
    lo = (a - hi.astype(F32)).astype(BF16)
    return hi, lo


_NN = (((1,), (0,)), ((), ()))
_NT = (((1,), (1,)), ((), ()))
_TN = (((0,), (0,)), ((), ()))


def _dg(a, b, dn):
    return lax.dot_general(a, b, dn, preferred_element_type=F32)


def _mm(a, b, precise, dn=_NN):
    if not precise:
        return _dg(a.astype(BF16), b.astype(BF16), dn)
    ah, al = _hi_lo(a.astype(F32))
    bh, bl = _hi_lo(b.astype(F32))
    return _dg(ah, bh, dn) + _dg(ah, bl, dn) + _dg(al, bh, dn)


def _mm_exact_rhs(a, b_exact, dn=_NN):
    ah, al = _hi_lo(a)
    return _dg(ah, b_exact, dn) + _dg(al, b_exact, dn)


def _mm_exact_lhs(a_exact, b, dn=_NN):
    bh, bl = _hi_lo(b)
    return _dg(a_exact, bh, dn) + _dg(a_exact, bl, dn)


def _softplus(z):
    return jnp.maximum(z, 0.0) + jnp.log(1.0 + jnp.exp(-jnp.abs(z)))


def _rms(x, w):
    return x * lax.rsqrt(jnp.mean(x * x, axis=-1, keepdims=True) + EPS) * w


def _inproj_kernel(x_ref, nw_ref, w_ref, cos_ref, sin_ref, o_ref, hh_ref, *maybe_hl, precise):
    j = pl.program_id(1)

    @pl.when(j == 0)
    def _():
        h = _rms(x_ref[...], nw_ref[...])
        hi = h.astype(BF16)
        hh_ref[...] = hi
        if precise:
            maybe_hl[0][...] = (h - hi.astype(F32)).astype(BF16)

    if precise:
        wh, wl = _hi_lo(w_ref[...])
        hh = hh_ref[...]
        acc = _dg(hh, wh, _NN) + _dg(hh, wl, _NN) + _dg(maybe_hl[0][...], wh, _NN)
    else:
        acc = _dg(hh_ref[...], w_ref[...], _NN)

    is_rope = jnp.logical_or(j == C_DQ // COL_TILE, j == C_DK // COL_TILE)

    @pl.when(is_rope)
    def _():
        lane = lax.broadcasted_iota(jnp.int32, acc.shape, 1)
        first = (lane % HEAD_DIM) < (HEAD_DIM // 2)
        partner = jnp.where(first, pltpu.roll(acc, COL_TILE - HEAD_DIM // 2, 1),
                            pltpu.roll(acc, HEAD_DIM // 2, 1))
        o_ref[...] = acc * cos_ref[...] + partner * sin_ref[...]

    @pl.when(jnp.logical_not(is_rope))
    def _():
        o_ref[...] = acc


def _inproj(x, norm_w, w, cos_t, sin_t, *, precise, tm):
    t = x.shape[0]
    n_pos_tiles = cos_t.shape[0] // tm
    scratch = [pltpu.VMEM((tm, D_MODEL), BF16)]
    if precise:
        scratch.append(pltpu.VMEM((tm, D_MODEL), BF16))
    return pl.pallas_call(
        functools.partial(_inproj_kernel, precise=precise),
        grid=(t // tm, PROJ_W // COL_TILE),
        in_specs=[
            pl.BlockSpec((tm, D_MODEL), lambda i, j: (i, 0)),
            pl.BlockSpec((1, D_MODEL), lambda i, j: (0, 0)),
            pl.BlockSpec((D_MODEL, COL_TILE), lambda i, j: (0, j)),
            pl.BlockSpec((tm, COL_TILE), lambda i, j: (i % n_pos_tiles, 0)),
            pl.BlockSpec((tm, COL_TILE), lambda i, j: (i % n_pos_tiles, 0)),
        ],
        out_specs=pl.BlockSpec((tm, COL_TILE), lambda i, j: (i, j)),
        out_shape=jax.ShapeDtypeStruct((t, PROJ_W), F32),
        scratch_shapes=scratch,
        compiler_params=_cparams(("parallel", "arbitrary")),
    )(x, norm_w, w, cos_t, sin_t)


def _sb_prompt_kernel(q_ref, k_ref, v_ref, o_ref, kb_ref, vb_ref, acc_ref, *, tq):
    qi = pl.program_id(2)
    lane_q = lax.broadcasted_iota(jnp.int32, (tq, LANES), 1)

    @pl.when(qi == 0)
    def _():
        lane = lax.broadcasted_iota(jnp.int32, (1, LANES), 1)
        v = v_ref[...]
        kb_ref[...] = k_ref[...].astype(BF16)
        vb_ref[0] = jnp.where(lane < HEAD_DIM, v, 0.0).astype(BF16)
        vb_ref[1] = jnp.where(lane >= HEAD_DIM, v, 0.0).astype(BF16)

    q = q_ref[...] * (1.0 / math.sqrt(HEAD_DIM))
    q_heads = (jnp.where(lane_q < HEAD_DIM, q, 0.0).astype(BF16),
               jnp.where(lane_q >= HEAD_DIM, q, 0.0).astype(BF16))
    row = lax.broadcasted_iota(jnp.int32, (tq, tq), 0)
    col = lax.broadcasted_iota(jnp.int32, (tq, tq), 1)
    later = jnp.where(row > col, 1.0, 0.0).astype(BF16)
    strictly_causal = col < row
    acc_ref[...] = jnp.zeros_like(acc_ref)

    def block(kb, carry, diagonal):
        start = pl.multiple_of(kb * tq, tq)
        k = kb_ref[pl.ds(start, tq), :]
        new_carry = []
        contrib = None
        for hd in range(2):
            z = _dg(q_heads[hd], k, _NT)
            sp = _softplus(z)
            if diagonal:
                sp = jnp.where(strictly_causal, sp, 0.0)
            after = _mm_exact_rhs(sp, later)
            total = after[:, 0:1] + sp[:, 0:1]
            w = jnp.exp(z - sp - after - carry[hd])
            if diagonal:
                w = jnp.where(strictly_causal, w, 0.0)
            pv = _dg(w.astype(BF16), vb_ref[hd, pl.ds(start, tq), :], _NN)
            contrib = pv if contrib is None else contrib + pv
            new_carry.append(carry[hd] + total)
        acc_ref[...] += contrib
        return tuple(new_carry)

    zero = jnp.zeros((tq, 1), F32)
    c0, c1 = block(qi, (zero, zero), True)

    def more(state):
        it, a, b = state
        return jnp.logical_and(it < qi, jnp.min(jnp.minimum(a, b)) < SB_UNDERFLOW)

    def step(state):
        it, a, b = state
        a, b = block(qi - 1 - it, (a, b), False)
        return it + 1, a, b

    lax.while_loop(more, step, (jnp.int32(0), c0, c1))
    o_ref[...] = acc_ref[...].astype(o_ref.dtype)


def _sb_prompt(proj, *, batch, seq, tq):
    nq = seq // tq
    t = batch * seq
    cq, ck, cv = C_SBQ // LANES, C_SBK // LANES, C_SBV // LANES
    return pl.pallas_call(
        functools.partial(_sb_prompt_kernel, tq=tq),
        grid=(batch, W_BRANCH // LANES, nq),
        in_specs=[
            pl.BlockSpec((tq, LANES), lambda b, hp, qi: (b * nq + qi, cq + hp)),
            pl.BlockSpec((seq, LANES), lambda b, hp, qi: (b, ck + hp)),
            pl.BlockSpec((seq, LANES), lambda b, hp, qi: (b, cv + hp)),
        ],
        out_specs=pl.BlockSpec((tq, LANES), lambda b, hp, qi: (b * nq + qi, hp)),
        out_shape=jax.ShapeDtypeStruct((t, W_BRANCH), BF16),
        scratch_shapes=[pltpu.VMEM((seq, LANES), BF16), pltpu.VMEM((2, seq, LANES), BF16),
                        pltpu.VMEM((tq, LANES), F32)],
        compiler_params=_cparams(("parallel", "parallel", "arbitrary")),
    )(proj, proj, proj)


def _lambda_value(lam_ref, lam_init):
    lv = lam_ref[...]
    s1 = jnp.sum(lv[0:1, :] * lv[1:2, :], axis=-1, keepdims=True)
    s2 = jnp.sum(lv[2:3, :] * lv[3:4, :], axis=-1, keepdims=True)
    return jnp.exp(s1) - jnp.exp(s2) + lam_init


def _diff_prompt_kernel(q_ref, k_ref, v_ref, lam_ref, nw_ref, o_ref, kb_ref, vb_ref, acc_ref, m_ref, l_ref,
                        *, tq, big, lam_init):
    qi = pl.program_id(2)

    @pl.when(qi == 0)
    def _():
        kb_ref[...] = k_ref[...].astype(BF16)
        vb_ref[...] = v_ref[...].astype(BF16)

    q = q_ref[...] * (LOG2E / math.sqrt(HEAD_DIM))
    lane_q = lax.broadcasted_iota(jnp.int32, (tq, LANES), 1)
    q_comp = (jnp.where(lane_q < HEAD_DIM, q, 0.0).astype(BF16),
              jnp.where(lane_q >= HEAD_DIM, q, 0.0).astype(BF16))
    row = lax.broadcasted_iota(jnp.int32, (tq, tq), 0)
    col = lax.broadcasted_iota(jnp.int32, (tq, tq), 1)
    causal = col <= row

    def block(start, size, diagonal, first):
        start = pl.multiple_of(start, tq)
        k = kb_ref[pl.ds(start, size), :]
        v = vb_ref[pl.ds(start, size), :]
        for c in range(2):
            s = _dg(q_comp[c], k, _NT)
            if diagonal:
                s = jnp.where(causal, s, -jnp.inf)
            blk_max = jnp.max(s, axis=-1, keepdims=True)
            if first:
                m_new = jnp.broadcast_to(blk_max, (tq, LANES))
            else:
                m_old = m_ref[c]
                m_new = jnp.maximum(m_old, blk_max)
                alpha = jnp.exp2(m_old - m_new)
            p = jnp.exp2(s - jnp.concatenate([m_new] * (size // LANES), axis=1))
            p_sum = jnp.sum(p, axis=-1, keepdims=True)
            pv = _dg(p.astype(BF16), v, _NN)
            if first:
                l_ref[c] = jnp.broadcast_to(p_sum, (tq, LANES))
                acc_ref[c] = pv
            else:
                l_ref[c] = alpha * l_ref[c] + p_sum
                acc_ref[c] = alpha * acc_ref[c] + pv
            m_ref[c] = m_new

    block(qi * tq, tq, True, True)
    n_big = (qi * tq) // big
    n_small = (qi * tq - n_big * big) // tq

    def big_body(it, _):
        block(it * big, big, False, False)
        return 0

    def small_body(it, _):
        block(n_big * big + it * tq, tq, False, False)
        return 0

    lax.fori_loop(0, n_big, big_body, 0)
    lax.fori_loop(0, n_small, small_body, 0)
    lam = _lambda_value(lam_ref, lam_init)
    o = acc_ref[0] / l_ref[0] - lam * (acc_ref[1] / l_ref[1])
    o_ref[...] = (_rms(o, nw_ref[...]) * (1.0 - lam_init)).astype(o_ref.dtype)


def _diff_prompt(proj, lam_vecs, diff_norm_w, *, batch, seq, tq, lam_init):
    nq = seq // tq
    t = batch * seq
    cq, ck, cv = C_DQ // LANES, C_DK // LANES, C_DV // LANES
    big = min(DIFF_KEY_CHUNK, seq)
    return pl.pallas_call(
        functools.partial(_diff_prompt_kernel, tq=tq, big=big, lam_init=lam_init),
        grid=(batch, W_BRANCH // LANES, nq),
        in_specs=[
            pl.BlockSpec((tq, LANES), lambda b, hh, qi: (b * nq + qi, cq + hh)),
            pl.BlockSpec((seq, LANES), lambda b, hh, qi: (b, ck + hh)),
            pl.BlockSpec((seq, LANES), lambda b, hh, qi: (b, cv + hh)),
            pl.BlockSpec((4, HEAD_DIM), lambda b, hh, qi: (0, 0)),
            pl.BlockSpec((1, LANES), lambda b, hh, qi: (0, 0)),
        ],
        out_specs=pl.BlockSpec((tq, LANES), lambda b, hh, qi: (b * nq + qi, hh)),
        out_shape=jax.ShapeDtypeStruct((t, W_BRANCH), BF16),
        scratch_shapes=[pltpu.VMEM((seq, LANES), BF16), pltpu.VMEM((seq, LANES), BF16),
                        pltpu.VMEM((2, tq, LANES), F32), pltpu.VMEM((2, tq, LANES), F32),
                        pltpu.VMEM((2, tq, LANES), F32)],
        compiler_params=_cparams(("parallel", "parallel", "arbitrary")),
    )(proj, proj, proj, lam_vecs, diff_norm_w)


def _ssd_prompt_kernel(z_ref, xa_ref, xb_ref, cw_ref, cb_ref, dtb_ref, alog_ref, dsk_ref, nw_ref,
                       y_ref, hout_ref, cout_ref, xpad_ref, h_ref, *, n_heads, n_groups):
    c = pl.program_id(1)
    nc = pl.num_programs(1)
    L = SSM_CHUNK
    P = HEAD_DIM
    N = SSM_STATE
    xw = n_heads * P
    gn = n_groups * N

    @pl.when(c == 0)
    def _():
        xpad_ref[0:8, :] = jnp.zeros((8, xpad_ref.shape[1]), F32)
        h_ref[...] = jnp.zeros_like(h_ref)

    xb = xb_ref[...]
    xbc = jnp.concatenate([xa_ref[...], xb[:, :2 * gn]], axis=1)
    xpad_ref[8:8 + L, :] = xbc
    conv = cb_ref[...]
    for tap in range(CONV_TAPS):
        conv = conv + xpad_ref[pl.ds(8 - (CONV_TAPS - 1) + tap, L), :] * cw_ref[tap:tap + 1, :]
    conv = conv * jax.nn.sigmoid(conv)
    xpad_ref[0:8, :] = xbc[L - 8:, :]

    @pl.when(c == nc - 1)
    def _():
        cout_ref[0] = xbc[L - (CONV_TAPS - 1):, :]

    xs = conv[:, :xw]
    bmat = conv[:, xw:xw + gn]
    cmat = conv[:, xw + gn:]

    lane = lax.broadcasted_iota(jnp.int32, (1, LANES), 1)
    head_lane = lane < n_heads
    dt = jnp.where(head_lane, _softplus(xb[:, 2 * gn:2 * gn + LANES] + dtb_ref[...]), 0.0)
    a = jnp.where(head_lane, -jnp.exp(alog_ref[...]), 0.0)
    dta = dt * a
    ri = lax.broadcasted_iota(jnp.int32, (L, L), 0)
    ci = lax.broadcasted_iota(jnp.int32, (L, L), 1)
    lower = jnp.where(ri >= ci, 1.0, 0.0).astype(BF16)
    dta_h, dta_l = _hi_lo(dta)
    dta_l2 = (dta - dta_h.astype(F32) - dta_l.astype(F32)).astype(BF16)
    cum = _dg(lower, dta_h, _NN) + _dg(lower, dta_l, _NN) + _dg(lower, dta_l2, _NN)
    cum_t = cum.T
    dt_t = dt.T
    causal = ri >= ci

    y_parts = []
    for g in range(n_groups):
        b_g = bmat[:, g * N:(g + 1) * N]
        c_g = cmat[:, g * N:(g + 1) * N]
        cb = _mm(c_g, b_g, False, _NT)
        for hh in range(n_heads // n_groups):
            hd = g * (n_heads // n_groups) + hh
            x_h = xs[:, hd * P:(hd + 1) * P]
            cum_col = cum[:, hd:hd + 1]
            cum_row = cum_t[hd:hd + 1, :]
            dt_col = dt[:, hd:hd + 1]
            dt_row = dt_t[hd:hd + 1, :]
            decay = jnp.exp(jnp.where(causal, cum_col - cum_row, -jnp.inf))
            y_intra = _mm(cb * decay * dt_row, x_h, False)
            h_prev = h_ref[hd]
            y_inter = _mm(c_g, h_prev, False, _NT) * jnp.exp(cum_col)
            cum_end = cum[L - 1:L, hd:hd + 1]
            w_end = jnp.exp(cum_end - cum_col) * dt_col
            s_chunk = _mm(x_h * w_end, b_g, False, _TN)
            h_ref[hd] = jnp.exp(cum_end) * h_prev + s_chunk
            y_parts.append(y_intra + y_inter + dsk_ref[0:1, hd:hd + 1] * x_h)
    y = jnp.concatenate(y_parts, axis=1)
    zg = z_ref[...]
    y = y * (zg * jax.nn.sigmoid(zg))
    gw = xw // n_groups
    nw = nw_ref[...]
    y = jnp.concatenate([_rms(y[:, g * gw:(g + 1) * gw], nw[:, g * gw:(g + 1) * gw])
                         for g in range(n_groups)], axis=1)
    y_ref[...] = y.astype(y_ref.dtype)

    @pl.when(c == nc - 1)
    def _():
        hout_ref[0] = h_ref[...]


def _ssd_prompt(proj, conv_w, conv_b, dt_bias, a_log, d_skip, norm_w, *, batch, seq, n_heads, n_groups):
    L = SSM_CHUNK
    nc = seq // L
    t = batch * seq
    cdim = conv_w.shape[1]
    cz, cx = C_Z // COL_TILE, C_XBC // COL_TILE
    full = lambda shape: pl.BlockSpec(shape, lambda b, c: (0,) * len(shape))
    return pl.pallas_call(
        functools.partial(_ssd_prompt_kernel, n_heads=n_heads, n_groups=n_groups),
        grid=(batch, nc),
        in_specs=[
            pl.BlockSpec((L, COL_TILE), lambda b, c: (b * nc + c, cz)),
            pl.BlockSpec((L, COL_TILE), lambda b, c: (b * nc + c, cx)),
            pl.BlockSpec((L, COL_TILE), lambda b, c: (b * nc + c, cx + 1)),
            full((CONV_TAPS, cdim)), full((1, cdim)), full((1, LANES)), full((1, LANES)),
            full((1, LANES)), full((1, W_BRANCH)),
        ],
        out_specs=[
            pl.BlockSpec((L, W_BRANCH), lambda b, c: (b * nc + c, 0)),
            pl.BlockSpec((1, n_heads, HEAD_DIM, SSM_STATE), lambda b, c: (b, 0, 0, 0)),
            pl.BlockSpec((1, CONV_TAPS - 1, cdim), lambda b, c: (b, 0, 0)),
        ],
        out_shape=[
            jax.ShapeDtypeStruct((t, W_BRANCH), BF16),
            jax.ShapeDtypeStruct((batch, n_heads, HEAD_DIM, SSM_STATE), F32),
            jax.ShapeDtypeStruct((batch, CONV_TAPS - 1, cdim), F32),
        ],
        scratch_shapes=[pltpu.VMEM((8 + L, cdim), F32), pltpu.VMEM((n_heads, HEAD_DIM, SSM_STATE), F32)],
        compiler_params=_cparams(("parallel", "arbitrary")),
    )(proj, proj, proj, conv_w, conv_b, dt_bias, a_log, d_skip, norm_w)


def _merge_kernel(x_ref, g0_ref, g1_ref, g2_ref, a_ref, b_ref, c_ref, wa_ref, wb_ref, wc_ref, wo_ref,
                  o_ref, *, precise):
    m = (jax.nn.sigmoid(g0_ref[...]) * _mm(a_ref[...], wa_ref[...], precise)
         + jax.nn.sigmoid(g1_ref[...]) * _mm(b_ref[...], wb_ref[...], precise)
         + jax.nn.sigmoid(g2_ref[...]) * _mm(c_ref[...], wc_ref[...], precise))
    o_ref[...] = x_ref[...] + _mm(m, wo_ref[...], precise)


def _merge(x, proj, o_sb, y_ssm, o_diff, w_sb, w_ssm, w_diff, w_o, *, precise, tm):
    t = x.shape[0]
    full = lambda shape: pl.BlockSpec(shape, lambda i: (0, 0))
    gate = lambda n: pl.BlockSpec((tm, D_MODEL), lambda i: (i, C_GATE // D_MODEL + n))
    branch = pl.BlockSpec((tm, W_BRANCH), lambda i: (i, 0))
    return pl.pallas_call(
        functools.partial(_merge_kernel, precise=precise),
        grid=(t // tm,),
        in_specs=[pl.BlockSpec((tm, D_MODEL), lambda i: (i, 0)), gate(0), gate(1), gate(2),
                  branch, branch, branch,
                  full((W_BRANCH, D_MODEL)), full((W_BRANCH, D_MODEL)), full((W_BRANCH, D_MODEL)),
                  full((D_MODEL, D_MODEL))],
        out_specs=pl.BlockSpec((tm, D_MODEL), lambda i: (i, 0)),
        out_shape=jax.ShapeDtypeStruct((t, D_MODEL), F32),
        compiler_params=_cparams(("parallel",)),
    )(x, proj, proj, proj, o_sb, y_ssm, o_diff, w_sb, w_ssm, w_diff, w_o)


def _route(logits):
    lane = lax.broadcasted_iota(jnp.int32, logits.shape, 1)
    neg = -jnp.inf
    big = jnp.int32(1 << 20)
    g = jnp.where(lane < MOE_GROUPS, logits, neg)
    g_max = jnp.max(g, axis=-1, keepdims=True)
    g_sel = jnp.min(jnp.where(g == g_max, lane, big), axis=-1, keepdims=True)
    g_w = 1.0 / jnp.sum(jnp.exp(g - g_max), axis=-1, keepdims=True)
    lo = MOE_GROUPS + g_sel * MOE_PER_GROUP
    e = jnp.where((lane >= lo) & (lane < lo + MOE_PER_GROUP), logits, neg)
    v1 = jnp.max(e, axis=-1, keepdims=True)
    i1 = jnp.min(jnp.where(e == v1, lane, big), axis=-1, keepdims=True)
    e2 = jnp.where(lane == i1, neg, e)
    v2 = jnp.max(e2, axis=-1, keepdims=True)
    i2 = jnp.min(jnp.where(e2 == v2, lane, big), axis=-1, keepdims=True)
    r = jnp.exp(v2 - v1)
    w1 = g_w / (1.0 + r)
    w2 = g_w * r / (1.0 + r)
    return jnp.where(lane == i1, w1, 0.0) + jnp.where(lane == i2, w2, 0.0)


def _moe_kernel(x_ref, nw_ref, wr_ref, wg_ref, wu_ref, wd_ref, fw_ref, o_ref,
                hh_ref, hl_ref, comb_ref, acc_ref, *, precise, final):
    e = pl.program_id(1)
    ne = pl.num_programs(1)

    @pl.when(e == 0)
    def _():
        h = _rms(x_ref[...], nw_ref[...])
        hi = h.astype(BF16)
        lo = (h - hi.astype(F32)).astype(BF16)
        hh_ref[...] = hi
        hl_ref[...] = lo
        wrh, wrl = _hi_lo(wr_ref[...])
        logits = _dg(hi, wrh, _NN) + _dg(hi, wrl, _NN) + _dg(lo, wrh, _NN)
        comb_ref[...] = _route(logits)
        acc_ref[...] = jnp.zeros_like(acc_ref)

    hh = hh_ref[...]
    if precise:
        hl = hl_ref[...]
        gh, gl = _hi_lo(wg_ref[0, 0])
        uh, ul = _hi_lo(wu_ref[0, 0])
        gate = _dg(hh, gh, _NN) + _dg(hh, gl, _NN) + _dg(hl, gh, _NN)
        up = _dg(hh, uh, _NN) + _dg(hh, ul, _NN) + _dg(hl, uh, _NN)
    else:
        gate = _dg(hh, wg_ref[0, 0], _NN)
        up = _dg(hh, wu_ref[0, 0], _NN)
    lane = lax.broadcasted_iota(jnp.int32, comb_ref.shape, 1)
    cw = jnp.sum(jnp.where(lane == e + MOE_GROUPS, comb_ref[...], 0.0), axis=-1, keepdims=True)
    he = gate * jax.nn.sigmoid(gate) * up * cw
    acc_ref[...] += _mm(he, wd_ref[0, 0], precise)

    @pl.when(e == ne - 1)
    def _():
        y = x_ref[...] + acc_ref[...]
        if final:
            y = _rms(y, fw_ref[...])
        o_ref[...] = y


def _moe(x, norm_w, w_route, w_gate, w_up, w_down, final_w, layer, *, precise, final, tm):
    t = x.shape[0]
    _, n_exp, _, ff = w_gate.shape
    return pl.pallas_call(
        functools.partial(_moe_kernel, precise=precise, final=final),
        grid=(t // tm, n_exp),
        in_specs=[
            pl.BlockSpec((tm, D_MODEL), lambda i, e: (i, 0)),
            pl.BlockSpec((1, D_MODEL), lambda i, e: (0, 0)),
            pl.BlockSpec((D_MODEL, LANES), lambda i, e: (0, 0)),
            pl.BlockSpec((1, 1, D_MODEL, ff), lambda i, e: (layer, e, 0, 0)),
            pl.BlockSpec((1, 1, D_MODEL, ff), lambda i, e: (layer, e, 0, 0)),
            pl.BlockSpec((1, 1, ff, D_MODEL), lambda i, e: (layer, e, 0, 0)),
            pl.BlockSpec((1, D_MODEL), lambda i, e: (0, 0)),
        ],
        out_specs=pl.BlockSpec((tm, D_MODEL), lambda i, e: (i, 0)),
        out_shape=jax.ShapeDtypeStruct((t, D_MODEL), F32),
        scratch_shapes=[pltpu.VMEM((tm, D_MODEL), BF16), pltpu.VMEM((tm, D_MODEL), BF16),
                        pltpu.VMEM((tm, LANES), F32), pltpu.VMEM((tm, D_MODEL), F32)],
        compiler_params=_cparams(("parallel", "arbitrary")),
    )(x, norm_w, w_route, w_gate, w_up, w_down, final_w)


def _row_to_col(row, eye):
    return jnp.sum(jnp.where(eye, jnp.broadcast_to(row, eye.shape), 0.0), axis=1, keepdims=True)


def _col_to_row(col, eye):
    return jnp.sum(jnp.where(eye, jnp.broadcast_to(col, eye.shape), 0.0), axis=0, keepdims=True)


def _sb_sample_kernel(*refs, pages_per_step, n_heads, resume):
    if resume:
        pt_ref, more_ref, q_ref, prev_o_ref, prev_c_ref = refs[:5]
        refs = refs[5:]
    else:
        pt_ref, q_ref = refs[:2]
        refs = refs[2:]
    k_refs = refs[:pages_per_step]
    v_refs = refs[pages_per_step:2 * pages_per_step]
    if resume:
        o_ref, qb_ref, acc_ref, c_ref = refs[2 * pages_per_step:]
    else:
        o_ref, c_out_ref, more_out_ref, qb_ref, acc_ref, c_ref = refs[2 * pages_per_step:]
    b = pl.program_id(0)
    j = pl.program_id(1)
    nj = pl.num_programs(1)
    P = HEAD_DIM
    page = qb_ref.shape[-1]
    eye = lax.broadcasted_iota(jnp.int32, (P, P), 0) == lax.broadcasted_iota(jnp.int32, (P, P), 1)

    @pl.when(j == 0)
    def _():
        q = q_ref[0] * (1.0 / math.sqrt(HEAD_DIM))
        for hd in range(n_heads):
            qb_ref[hd] = jnp.broadcast_to(_row_to_col(q[:, hd * P:(hd + 1) * P], eye), (P, page))
        acc_ref[...] = jnp.zeros_like(acc_ref)
        c_ref[...] = prev_c_ref[0] if resume else jnp.zeros_like(c_ref)

    def walk():
        ri = lax.broadcasted_iota(jnp.int32, (page, page), 0)
        ci = lax.broadcasted_iota(jnp.int32, (page, page), 1)
        later = jnp.where(ri > ci, 1.0, 0.0).astype(BF16)
        carry = c_ref[:, 0:1]
        for p in range(pages_per_step):
            z = jnp.concatenate([jnp.sum(k_refs[p][0, 0, hd] * qb_ref[hd], axis=0, keepdims=True)
                                 for hd in range(n_heads)], axis=0)
            sp = _softplus(z)
            sph, spl = _hi_lo(sp)
            spl2 = (sp - sph.astype(F32) - spl.astype(F32)).astype(BF16)
            after = _dg(sph, later, _NN) + _dg(spl, later, _NN) + _dg(spl2, later, _NN)
            w = jnp.exp(z - sp - after - carry)
            for hd in range(n_heads):
                acc_ref[hd] += v_refs[p][0, 0, hd] * w[hd:hd + 1, :]
            carry = carry + after[:, 0:1] + sp[:, 0:1]
        c_ref[...] = jnp.broadcast_to(carry, c_ref.shape)

    if resume:
        pl.when(more_ref[b] != 0)(walk)
    else:
        walk()

    @pl.when(j == nj - 1)
    def _():
        o = jnp.concatenate(
            [_col_to_row(jnp.sum(acc_ref[hd], axis=1, keepdims=True), eye) for hd in range(n_heads)], axis=1)
        if resume:
            o_ref[0] = o + prev_o_ref[0]
        else:
            o_ref[0] = o
            carry = c_ref[...]
            c_out_ref[0] = carry
            more_out_ref[0] = jnp.broadcast_to(jnp.where(jnp.min(carry) < SB_UNDERFLOW, 1, 0).astype(jnp.int32),
                                               (1, LANES))


def _sb_sample_walk(page_table, q, cache_k_t, cache_v_t, layer, newest, n_steps, resume_args, *, pages_per_step):
    nb, n_pages = page_table.shape
    _, _, n_heads, hdim, page = cache_k_t.shape
    width = n_heads * hdim
    pt_flat = page_table.reshape(-1)
    resume = resume_args is not None

    def page_spec(p):
        def index(b, j, pt, *more):
            phys = pt[b * n_pages + newest - j * pages_per_step - p]
            if resume:
                phys = jnp.where(more[0][b] != 0, phys, 0)
            return (layer, phys, 0, 0, 0)
        return pl.BlockSpec((1, 1, n_heads, hdim, page), index)

    specs = [page_spec(p) for p in range(pages_per_step)]
    row = pl.BlockSpec((1, 1, width), lambda b, j, *_: (b, 0, 0))
    carry_spec = pl.BlockSpec((1, n_heads, LANES), lambda b, j, *_: (b, 0, 0))
    flag_spec = pl.BlockSpec((1, 1, LANES), lambda b, j, *_: (b, 0, 0))
    scratch = [pltpu.VMEM((n_heads, hdim, page), F32), pltpu.VMEM((n_heads, hdim, page), F32),
               pltpu.VMEM((n_heads, LANES), F32)]
    kernel_fn = functools.partial(_sb_sample_kernel, pages_per_step=pages_per_step, n_heads=n_heads, resume=resume)
    pages = [cache_k_t] * pages_per_step + [cache_v_t] * pages_per_step
    if resume:
        more, prev_o, prev_c = resume_args
        grid_spec = pltpu.PrefetchScalarGridSpec(
            num_scalar_prefetch=2, grid=(nb, n_steps),
            in_specs=[row, row, carry_spec] + specs + specs, out_specs=row, scratch_shapes=scratch)
        return pl.pallas_call(
            kernel_fn, grid_spec=grid_spec, out_shape=jax.ShapeDtypeStruct((nb, 1, width), F32),
            compiler_params=_cparams(("parallel", "arbitrary")),
        )(pt_flat, more, q, prev_o, prev_c, *pages)
    grid_spec = pltpu.PrefetchScalarGridSpec(
        num_scalar_prefetch=1, grid=(nb, n_steps),
        in_specs=[row] + specs + specs, out_specs=[row, carry_spec, flag_spec], scratch_shapes=scratch)
    return pl.pallas_call(
        kernel_fn, grid_spec=grid_spec,
        out_shape=[jax.ShapeDtypeStruct((nb, 1, width), F32), jax.ShapeDtypeStruct((nb, n_heads, LANES), F32),
                   jax.ShapeDtypeStruct((nb, 1, LANES), jnp.int32)],
        compiler_params=_cparams(("parallel", "arbitrary")),
    )(pt_flat, q, *pages)


def _sb_sample(page_table, q, cache_k_t, cache_v_t, layer, *, pages_per_step):
    n_pages = page_table.shape[1]
    o, carry, more = _sb_sample_walk(page_table, q, cache_k_t, cache_v_t, layer, n_pages - 1, 1, None,
                                     pages_per_step=pages_per_step)
    n_steps = n_pages // pages_per_step - 1
    if n_steps == 0:
        return o
    return _sb_sample_walk(page_table, q, cache_k_t, cache_v_t, layer, n_pages - 1 - pages_per_step, n_steps,
                           (more[:, 0, 0], o, carry), pages_per_step=pages_per_step)


def _diff_sample_kernel(pt_ref, q_ref, kn_ref, vn_ref, lam_ref, nw_ref, *refs, pages_per_step, n_heads, lam_init):
    k_refs = refs[:pages_per_step]
    v_refs = refs[pages_per_step:2 * pages_per_step]
    o_ref, q8_ref, acc_ref, m_ref, l_ref = refs[2 * pages_per_step:]
    j = pl.program_id(1)
    nj = pl.num_programs(1)
    rows = k_refs[0].shape[2]
    comp_of_lane = lax.broadcasted_iota(jnp.int32, (LANES, LANES), 0) // HEAD_DIM
    sel = [jnp.where(comp_of_lane == c, 1.0, 0.0).astype(BF16) for c in range(2)]
    row8 = lax.broadcasted_iota(jnp.int32, (8, LANES), 0)

    def per_sublane(vec):
        return jnp.concatenate([vec[:, (i % n_heads) * LANES:(i % n_heads + 1) * LANES] for i in range(8)], axis=0)

    @pl.when(j == 0)
    def _():
        q8 = per_sublane(q_ref[0]) * (LOG2E / math.sqrt(HEAD_DIM))
        q8_ref[...] = q8
        prod = per_sublane(kn_ref[0]) * q8
        v_new = jnp.where(row8 < n_heads, per_sublane(vn_ref[0]), 0.0)
        for c in range(2):
            m_ref[c] = _mm_exact_rhs(prod, sel[c])
            l_ref[c] = jnp.where(row8 < n_heads, 1.0, 0.0)
            acc_ref[c] = v_new

    q8 = q8_ref[...]
    sel_both = jnp.concatenate([jnp.concatenate(sel, axis=1)] * 2, axis=0)
    for p in range(pages_per_step):
        k3 = k_refs[p][0, 0].reshape(rows // 8, 8, LANES)
        v3 = v_refs[p][0, 0].reshape(rows // 8, 8, LANES)
        prod_hi, prod_lo = _hi_lo((k3 * q8[None]).reshape(rows, LANES))
        s_both = _dg(jnp.concatenate([prod_hi, prod_lo], axis=1), sel_both, _NN)
        for c in range(2):
            s = s_both[:, c * LANES:(c + 1) * LANES].reshape(rows // 8, 8, LANES)
            blk_max = jnp.max(s, axis=0)
            blk_max = jnp.maximum(blk_max, pltpu.roll(blk_max, n_heads, 0))
            m_old = m_ref[c]
            m_new = jnp.maximum(m_old, blk_max)
            alpha = jnp.exp2(m_old - m_new)
            pr = jnp.exp2(s - m_new[None])
            l_ref[c] = alpha * l_ref[c] + jnp.sum(pr, axis=0)
            acc_ref[c] = alpha * acc_ref[c] + jnp.sum(pr * v3, axis=0)
            m_ref[c] = m_new

    @pl.when(j == nj - 1)
    def _():
        lam = _lambda_value(lam_ref, lam_init)
        outs = []
        for c in range(2):
            l = l_ref[c]
            a = acc_ref[c]
            outs.append((a + pltpu.roll(a, n_heads, 0)) / (l + pltpu.roll(l, n_heads, 0)))
        o = _rms(outs[0] - lam * outs[1], nw_ref[...]) * (1.0 - lam_init)
        o_ref[0] = jnp.concatenate([o[hd:hd + 1, :] for hd in range(n_heads)], axis=1)


def _diff_sample(page_table, q, k_new, v_new, lam_vecs, diff_norm_w, cache_k_r, cache_v_r, layer, *,
                 pages_per_step, n_heads, lam_init):
    nb, n_pages = page_table.shape
    rows = cache_k_r.shape[2]
    width = n_heads * LANES
    nj = n_pages // pages_per_step
    pt_flat = page_table.reshape(-1)

    def page_spec(p):
        def index(b, j, pt):
            return (layer, pt[b * n_pages + j * pages_per_step + p], 0, 0)
        return pl.BlockSpec((1, 1, rows, LANES), index)

    specs = [page_spec(p) for p in range(pages_per_step)]
    row = pl.BlockSpec((1, 1, width), lambda b, j, pt: (b, 0, 0))
    grid_spec = pltpu.PrefetchScalarGridSpec(
        num_scalar_prefetch=1,
        grid=(nb, nj),
        in_specs=[row, row, row,
                  pl.BlockSpec((4, HEAD_DIM), lambda b, j, pt: (0, 0)),
                  pl.BlockSpec((1, LANES), lambda b, j, pt: (0, 0))] + specs + specs,
        out_specs=row,
        scratch_shapes=[pltpu.VMEM((8, LANES), F32), pltpu.VMEM((2, 8, LANES), F32),
                        pltpu.VMEM((2, 8, LANES), F32), pltpu.VMEM((2, 8, LANES), F32)],
    )
    return pl.pallas_call(
        functools.partial(_diff_sample_kernel, pages_per_step=pages_per_step, n_heads=n_heads, lam_init=lam_init),
        grid_spec=grid_spec,
        out_shape=jax.ShapeDtypeStruct((nb, 1, width), F32),
        compiler_params=_cparams(("parallel", "arbitrary")),
    )(pt_flat, q, k_new, v_new, lam_vecs, diff_norm_w,
      *([cache_k_r] * pages_per_step), *([cache_v_r] * pages_per_step))


def _ssd_sample_kernel(z_ref, xa_ref, xb_ref, cprev_ref, hprev_ref, cw_ref, cb_ref, dtb_ref, alog_ref, dsk_ref,
                       nw_ref, y_ref, hout_ref, cout_ref, *, n_heads, n_groups):
    P = HEAD_DIM
    N = SSM_STATE
    xw = n_heads * P
    gn = n_groups * N
    xb = xb_ref[0]
    xbc = jnp.concatenate([xa_ref[0], xb[:, :2 * gn]], axis=1)
    prev = cprev_ref[0]
    conv = cb_ref[...] + xbc * cw_ref[CONV_TAPS - 1:CONV_TAPS, :]
    for tap in range(CONV_TAPS - 1):
        conv = conv + prev[tap:tap + 1, :] * cw_ref[tap:tap + 1, :]
    conv = conv * jax.nn.sigmoid(conv)
    cout_ref[0, 0:CONV_TAPS - 2, :] = prev[1:, :]
    cout_ref[0, CONV_TAPS - 2:CONV_TAPS - 1, :] = xbc
    xs = conv[:, :xw]
    dt = _softplus(xb[:, 2 * gn:2 * gn + LANES] + dtb_ref[...])
    a = -jnp.exp(alog_ref[...])
    decay = jnp.exp(dt * a)
    eye = lax.broadcasted_iota(jnp.int32, (P, P), 0) == lax.broadcasted_iota(jnp.int32, (P, P), 1)
    y_parts = []
    for hd in range(n_heads):
        g = hd // (n_heads // n_groups)
        b_g = conv[:, xw + g * N:xw + (g + 1) * N]
        c_g = conv[:, xw + gn + g * N:xw + gn + (g + 1) * N]
        x_h = xs[:, hd * P:(hd + 1) * P]
        x_col = jnp.sum(jnp.where(eye, jnp.broadcast_to(x_h, (P, P)), 0.0), axis=1, keepdims=True)
        h_new = decay[:, hd:hd + 1] * hprev_ref[0, hd] + (dt[:, hd:hd + 1] * x_col) * b_g
        hout_ref[0, hd] = h_new
        y_col = jnp.sum(h_new * c_g, axis=1, keepdims=True)
        y_row = jnp.sum(jnp.where(eye, jnp.broadcast_to(y_col, (P, P)), 0.0), axis=0, keepdims=True)
        y_parts.append(y_row + dsk_ref[0:1, hd:hd + 1] * x_h)
    y = jnp.concatenate(y_parts, axis=1)
    zg = z_ref[0]
    y = y * (zg * jax.nn.sigmoid(zg))
    gw = xw // n_groups
    nw = nw_ref[...]
    y_ref[0] = jnp.concatenate([_rms(y[:, g * gw:(g + 1) * gw], nw[:, g * gw:(g + 1) * gw])
                                for g in range(n_groups)], axis=1)


def _ssd_sample(proj3, conv_prev, ssm_prev, conv_w, conv_b, dt_bias, a_log, d_skip, norm_w, *, n_heads, n_groups):
    nb = proj3.shape[0]
    cdim = conv_w.shape[1]
    cz, cx = C_Z // COL_TILE, C_XBC // COL_TILE
    full = lambda shape: pl.BlockSpec(shape, lambda b: (0,) * len(shape))
    col = lambda cidx: pl.BlockSpec((1, 1, COL_TILE), lambda b: (b, 0, cidx))
    return pl.pallas_call(
        functools.partial(_ssd_sample_kernel, n_heads=n_heads, n_groups=n_groups),
        grid=(nb,),
        in_specs=[col(cz), col(cx), col(cx + 1),
                  pl.BlockSpec((1, CONV_TAPS - 1, cdim), lambda b: (b, 0, 0)),
                  pl.BlockSpec((1, n_heads, HEAD_DIM, SSM_STATE), lambda b: (b, 0, 0, 0)),
                  full((CONV_TAPS, cdim)), full((1, cdim)), full((1, LANES)), full((1, LANES)),
                  full((1, LANES)), full((1, W_BRANCH))],
        out_specs=[pl.BlockSpec((1, 1, W_BRANCH), lambda b: (b, 0, 0)),
                   pl.BlockSpec((1, n_heads, HEAD_DIM, SSM_STATE), lambda b: (b, 0, 0, 0)),
                   pl.BlockSpec((1, CONV_TAPS - 1, cdim), lambda b: (b, 0, 0))],
        out_shape=[jax.ShapeDtypeStruct((nb, 1, W_BRANCH), F32),
                   jax.ShapeDtypeStruct((nb, n_heads, HEAD_DIM, SSM_STATE), F32),
                   jax.ShapeDtypeStruct((nb, CONV_TAPS - 1, cdim), F32)],
        compiler_params=_cparams(("parallel",)),
    )(proj3, proj3, proj3, conv_prev, ssm_prev, conv_w, conv_b, dt_bias, a_log, d_skip, norm_w)


def _rope_tables(pos):
    half = HEAD_DIM // 2
    inv_freq = ROPE_THETA ** (-jnp.arange(half, dtype=F32) / half)
    ang = pos.astype(F32)[:, None] * inv_freq[None, :]
    cos, sin = jnp.cos(ang), jnp.sin(ang)
    reps = COL_TILE // HEAD_DIM
    cos_t = jnp.tile(jnp.concatenate([cos, cos], axis=1), (1, reps))
    sin_t = jnp.tile(jnp.concatenate([-sin, sin], axis=1), (1, reps))
    return cos_t, sin_t


def _rearrange_w_in(w):
    n_gate = 3 * D_MODEL
    split = C_DT - C_SBQ + 8
    body = w[:, :w.shape[1] - n_gate]
    pad = jnp.zeros((w.shape[0], C_DQ - C_SBQ - split), w.dtype)
    return jnp.concatenate([w[:, w.shape[1] - n_gate:], body[:, :split], pad, body[:, split:]], axis=1)


def _pad_lanes(v):
    v = v.reshape(1, -1)
    return jnp.pad(v, ((0, 0), (0, LANES - v.shape[1])))


def kernel(x_prompt, x_sample, cache_sb_k, cache_sb_v, cache_diff_k, cache_diff_v, state_ssm, state_conv,
           page_table, norm1_w, w_in, conv_w, conv_b, dt_bias, a_log, d_skip, ssm_norm_w,
           lam_q1, lam_k1, lam_q2, lam_k2, diff_norm_w, w_sb_out, w_ssm_out, w_diff_out, w_o,
           norm2_w, w_group, w_expert, w_gate, w_up, w_down, final_norm_w):
    bp, seq, d = x_prompt.shape
    bs = x_sample.shape[0]
    depth = w_in.shape[0]
    n_pages = page_table.shape[1]
    page = cache_sb_k.shape[2]
    sb_heads, diff_heads = cache_sb_k.shape[3], cache_diff_k.shape[3]
    n_heads = state_ssm.shape[2]
    cdim = state_conv.shape[3]
    n_groups = (cdim - n_heads * HEAD_DIM) // (2 * SSM_STATE)
    tp = bp * seq
    tq = min(256, seq)
    tq_diff = min(512, seq)
    w_gate_b, w_up_b, w_down_b = w_gate.astype(BF16), w_up.astype(BF16), w_down.astype(BF16)
    tm_proj = min(1024, seq)
    tm_merge = min(512, seq)
    tm_moe = min(1024, seq)
    pages_per_step = math.gcd(n_pages, 8)

    sb_k_t = jnp.transpose(cache_sb_k, (0, 1, 3, 4, 2))
    sb_v_t = jnp.transpose(cache_sb_v, (0, 1, 3, 4, 2))
    diff_k_r = cache_diff_k.reshape(depth, cache_diff_k.shape[1], page * diff_heads, 2 * HEAD_DIM)
    diff_v_r = cache_diff_v.reshape(depth, cache_diff_v.shape[1], page * diff_heads, 2 * HEAD_DIM)

    cos_p, sin_p = _rope_tables(jnp.arange(seq))
    cos_s, sin_s = _rope_tables(jnp.full((bs,), n_pages * page, jnp.int32))

    xp = x_prompt.reshape(tp, d)
    xs = x_sample.reshape(bs, d)
    outs = {k: [] for k in ("pk", "pv", "pdk", "pdv", "pssm", "pconv", "sk", "sv", "sdk", "sdv", "sssm", "sconv")}
    row = lambda v: v.reshape(1, -1)
    for l in range(depth):
        lam_init = 0.8 - 0.6 * math.exp(-0.3 * l)
        last = l == depth - 1
        w_in_f = _rearrange_w_in(w_in[l])
        lam_vecs = jnp.stack([lam_q1[l], lam_k1[l], lam_q2[l], lam_k2[l]])
        w_route = jnp.pad(jnp.concatenate([w_group[l], w_expert[l]], axis=1),
                          ((0, 0), (0, LANES - MOE_GROUPS - MOE_GROUPS * MOE_PER_GROUP)))
        ssd_w = (conv_w[l], row(conv_b[l]), _pad_lanes(dt_bias[l]), _pad_lanes(a_log[l]), _pad_lanes(d_skip[l]),
                 row(ssm_norm_w[l]))

        proj = _inproj(xp, row(norm1_w[l]), w_in_f.astype(BF16), cos_p, sin_p, precise=False, tm=tm_proj)
        o_sb = _sb_prompt(proj, batch=bp, seq=seq, tq=tq)
        o_diff = _diff_prompt(proj, lam_vecs, row(diff_norm_w[l]), batch=bp, seq=seq, tq=tq_diff,
                              lam_init=lam_init)
        y_ssm, ssm_p, conv_p = _ssd_prompt(proj, *ssd_w, batch=bp, seq=seq, n_heads=n_heads, n_groups=n_groups)
        xp = _merge(xp, proj, o_sb, y_ssm, o_diff, w_sb_out[l].astype(BF16), w_ssm_out[l].astype(BF16),
                    w_diff_out[l].astype(BF16), w_o[l].astype(BF16), precise=False, tm=tm_merge)
        xp = _moe(xp, row(norm2_w[l]), w_route, w_gate_b, w_up_b, w_down_b, row(final_norm_w), l,
                  precise=False, final=last, tm=tm_moe)
        outs["pk"].append(proj[:, C_SBK:C_SBK + W_BRANCH].reshape(bp, seq, sb_heads, HEAD_DIM))
        outs["pv"].append(proj[:, C_SBV:C_SBV + W_BRANCH].reshape(bp, seq, sb_heads, HEAD_DIM))
        outs["pdk"].append(proj[:, C_DK:C_DK + W_BRANCH].reshape(bp, seq, diff_heads, 2 * HEAD_DIM))
        outs["pdv"].append(proj[:, C_DV:C_DV + W_BRANCH].reshape(bp, seq, diff_heads, 2 * HEAD_DIM))
        outs["pssm"].append(ssm_p)
        outs["pconv"].append(conv_p)

        proj_s = _inproj(xs, row(norm1_w[l]), w_in_f, cos_s, sin_s, precise=True, tm=bs)
        proj3 = proj_s.reshape(bs, 1, PROJ_W)
        sec = lambda c0: proj3[:, :, c0:c0 + W_BRANCH]
        o_sb_s = _sb_sample(page_table, sec(C_SBQ), sb_k_t, sb_v_t, l, pages_per_step=pages_per_step)
        o_diff_s = _diff_sample(page_table, sec(C_DQ), sec(C_DK), sec(C_DV), lam_vecs, row(diff_norm_w[l]),
                                diff_k_r, diff_v_r, l, pages_per_step=pages_per_step, n_heads=diff_heads,
                                lam_init=lam_init)
        y_ssm_s, ssm_s, conv_s = _ssd_sample(proj3, state_conv[l], state_ssm[l], *ssd_w,
                                             n_heads=n_heads, n_groups=n_groups)
        xs = _merge(xs, proj_s, o_sb_s.reshape(bs, -1), y_ssm_s.reshape(bs, -1), o_diff_s.reshape(bs, -1),
                    w_sb_out[l], w_ssm_out[l], w_diff_out[l], w_o[l], precise=True, tm=bs)
        xs = _moe(xs, row(norm2_w[l]), w_route, w_gate, w_up, w_down, row(final_norm_w), l,
                  precise=True, final=last, tm=bs)
        outs["sk"].append(proj_s[:, C_SBK:C_SBK + W_BRANCH].reshape(bs, 1, sb_heads, HEAD_DIM))
        outs["sv"].append(proj_s[:, C_SBV:C_SBV + W_BRANCH].reshape(bs, 1, sb_heads, HEAD_DIM))
        outs["sdk"].append(proj_s[:, C_DK:C_DK + W_BRANCH].reshape(bs, 1, diff_heads, 2 * HEAD_DIM))
        outs["sdv"].append(proj_s[:, C_DV:C_DV + W_BRANCH].reshape(bs, 1, diff_heads, 2 * HEAD_DIM))
        outs["sssm"].append(ssm_s)
        outs["sconv"].append(conv_s)

    st = {k: jnp.stack(v) for k, v in outs.items()}
    return (xp.reshape(bp, seq, d), xs.reshape(bs, 1, d),
            st["pk"], st["pv"], st["pdk"], st["pdv"], st["pssm"], st["pconv"],
            st["sk"], st["sv"], st["sdk"], st["sdv"], st["sssm"], st["sconv"])
```

```python
import functools
import math

import jax
import jax.numpy as jnp
from jax import lax
from jax.experimental import pallas as pl
from jax.experimental.pallas import tpu as pltpu

F32 = jnp.float32
BF16 = jnp.bfloat16

EPS = 1e-6
ROPE_THETA = 10000.0
LANES = 128
HEAD_DIM = 64
SSM_STATE = 64
SSM_CHUNK = 128
CONV_TAPS = 4
MOE_GROUPS = 4
MOE_PER_GROUP = 4
VMEM_LIMIT = 48 * 1024 * 1024
LOG2E = 1.4426950408889634
DIFF_KEY_CHUNK = 1024
SB_UNDERFLOW = 104.0

D_MODEL = 1024
W_BRANCH = 512
C_GATE = 0
C_SBQ = 3072
C_Z = 3584
C_XBC = 4096
C_DT = 4864
C_DQ = 5120
PROMPT_W = 5632
C_SBK = 5632
C_SBV = 6144
C_DK = 6656
C_DV = 7168
PROJ_W = 7680
COL_TILE = 512


def _cparams(sem):
    return pltpu.CompilerParams(dimension_semantics=sem, vmem_limit_bytes=VMEM_LIMIT)


def _hi_lo(a):
    hi = a.astype(BF16)
    lo = (a - hi.astype(F32)).astype(BF16)
    return hi, lo


_NN = (((1,), (0,)), ((), ()))
_NT = (((1,), (1,)), ((), ()))
_TN = (((0,), (0,)), ((), ()))


def _dg(a, b, dn):
    return lax.dot_general(a, b, dn, preferred_element_type=F32)


def _mm(a, b, precise, dn=_NN):
    if not precise:
        return _dg(a.astype(BF16), b.astype(BF16), dn)
    ah, al = _hi_lo(a.astype(F32))
    bh, bl = _hi_lo(b.astype(F32))
    return _dg(ah, bh, dn) + _dg(ah, bl, dn) + _dg(al, bh, dn)


def _mm_exact_rhs(a, b_exact, dn=_NN):
    ah, al = _hi_lo(a)
    return _dg(ah, b_exact, dn) + _dg(al, b_exact, dn)


def _mm_exact_lhs(a_exact, b, dn=_NN):
    bh, bl = _hi_lo(b)
    return _dg(a_exact, bh, dn) + _dg(a_exact, bl, dn)


def _softplus(z):
    return jnp.maximum(z, 0.0) + jnp.log(1.0 + jnp.exp(-jnp.abs(z)))


def _rms(x, w):
    return x * lax.rsqrt(jnp.mean(x * x, axis=-1, keepdims=True) + EPS) * w


def _rotary(acc, cos_ref, sin_ref):
    reps = acc.shape[1] // LANES
    cos = jnp.concatenate([cos_ref[...]] * reps, axis=1)
    sin = jnp.concatenate([sin_ref[...]] * reps, axis=1)
    lane = lax.broadcasted_iota(jnp.int32, acc.shape, 1)
    first = (lane % HEAD_DIM) < (HEAD_DIM // 2)
    partner = jnp.where(first, pltpu.roll(acc, acc.shape[1] - HEAD_DIM // 2, 1),
                        pltpu.roll(acc, HEAD_DIM // 2, 1))
    return acc * cos + partner * sin


def _inproj_kernel(x_ref, nw_ref, w_ref, cos_ref, sin_ref, o_ref, hh_ref, hl_ref):
    j = pl.program_id(1)

    @pl.when(j == 0)
    def _():
        h = _rms(x_ref[...], nw_ref[...])
        hi = h.astype(BF16)
        hh_ref[...] = hi
        hl_ref[...] = (h - hi.astype(F32)).astype(BF16)

    wh, wl = _hi_lo(w_ref[...])
    hh = hh_ref[...]
    acc = _dg(hh, wh, _NN) + _dg(hh, wl, _NN) + _dg(hl_ref[...], wh, _NN)
    is_rope = jnp.logical_or(j == C_DQ // COL_TILE, j == C_DK // COL_TILE)

    @pl.when(is_rope)
    def _():
        o_ref[...] = _rotary(acc, cos_ref, sin_ref)

    @pl.when(jnp.logical_not(is_rope))
    def _():
        o_ref[...] = acc


def _inproj(x, norm_w, w, cos_t, sin_t):
    t = x.shape[0]
    return pl.pallas_call(
        _inproj_kernel,
        grid=(1, PROJ_W // COL_TILE),
        in_specs=[
            pl.BlockSpec((t, D_MODEL), lambda i, j: (0, 0)),
            pl.BlockSpec((1, D_MODEL), lambda i, j: (0, 0)),
            pl.BlockSpec((D_MODEL, COL_TILE), lambda i, j: (0, j)),
            pl.BlockSpec((t, LANES), lambda i, j: (0, 0)),
            pl.BlockSpec((t, LANES), lambda i, j: (0, 0)),
        ],
        out_specs=pl.BlockSpec((t, COL_TILE), lambda i, j: (0, j)),
        out_shape=jax.ShapeDtypeStruct((t, PROJ_W), F32),
        scratch_shapes=[pltpu.VMEM((t, D_MODEL), BF16), pltpu.VMEM((t, D_MODEL), BF16)],
        compiler_params=_cparams(("parallel", "arbitrary")),
    )(x, norm_w, w, cos_t, sin_t)


_T_SBK, _T_SBV, _T_DK, _T_DV = (c // COL_TILE for c in (C_SBK, C_SBV, C_DK, C_DV))


def _inproj_prompt_kernel(x_ref, nw_ref, w_ref, wt_ref, cos_ref, sin_ref, *refs):
    proj_ref, kt_ref, vt_ref, dk_ref, dv_ref, hh_ref = refs[-6:]
    j = pl.program_id(1)

    @pl.when(j == 0)
    def _():
        hh_ref[...] = _rms(x_ref[...], nw_ref[...]).astype(BF16)

    transposed = jnp.logical_or(j == _T_SBK, j == _T_SBV)

    @pl.when(jnp.logical_not(transposed))
    def _():
        acc = _dg(hh_ref[...], w_ref[...], _NN)

        @pl.when(j == C_DQ // COL_TILE)
        def _():
            proj_ref[...] = _rotary(acc, cos_ref, sin_ref)

        @pl.when(jnp.logical_and(j < PROMPT_W // COL_TILE, j != C_DQ // COL_TILE))
        def _():
            proj_ref[...] = acc

        @pl.when(j == _T_DK)
        def _():
            dk_ref[0] = _rotary(acc, cos_ref, sin_ref)

        @pl.when(j == _T_DV)
        def _():
            dv_ref[0] = acc

    @pl.when(j == _T_SBK)
    def _():
        kt_ref[0, 0] = _dg(wt_ref[...], hh_ref[...], _NT)

    @pl.when(j == _T_SBV)
    def _():
        vt_ref[0, 0] = _dg(wt_ref[...], hh_ref[...], _NT)


def _inproj_prompt(x, norm_w, w, w_kv_t, cos_t, sin_t, rows, layer, *, depth, batch, seq, tm):
    t = x.shape[0]
    nt = seq // tm
    last_proj = PROMPT_W // COL_TILE - 1
    any_spec = pl.BlockSpec(memory_space=pl.ANY)
    t_spec = pl.BlockSpec((1, 1, W_BRANCH, tm), lambda i, j: (layer, i // nt, 0, i % nt))
    r_spec = pl.BlockSpec((1, tm, W_BRANCH), lambda i, j: (layer, i, 0))
    t_shape = jax.ShapeDtypeStruct((depth, batch, W_BRANCH, seq), F32)
    r_shape = jax.ShapeDtypeStruct((depth, t, W_BRANCH), F32)
    n_fixed = 6
    return pl.pallas_call(
        _inproj_prompt_kernel,
        grid=(t // tm, PROJ_W // COL_TILE),
        in_specs=[
            pl.BlockSpec((tm, D_MODEL), lambda i, j: (i, 0)),
            pl.BlockSpec((1, D_MODEL), lambda i, j: (0, 0)),
            pl.BlockSpec((D_MODEL, COL_TILE),
                         lambda i, j: (0, jnp.where(j <= last_proj, j, jnp.where(j < _T_DK, last_proj, j)))),
            pl.BlockSpec((COL_TILE, D_MODEL), lambda i, j: (jnp.clip(j - _T_SBK, 0, 1), 0)),
            pl.BlockSpec((tm, LANES), lambda i, j: (i % nt, 0)),
            pl.BlockSpec((tm, LANES), lambda i, j: (i % nt, 0)),
        ] + ([] if rows is None else [any_spec] * 4),
        out_specs=[pl.BlockSpec((tm, COL_TILE), lambda i, j: (i, jnp.minimum(j, last_proj))),
                   t_spec, t_spec, r_spec, r_spec],
        out_shape=[jax.ShapeDtypeStruct((t, PROMPT_W), F32), t_shape, t_shape, r_shape, r_shape],
        input_output_aliases={} if rows is None else {n_fixed + k: 1 + k for k in range(4)},
        scratch_shapes=[pltpu.VMEM((tm, D_MODEL), BF16)],
        compiler_params=_cparams(("parallel", "arbitrary")),
    )(x, norm_w, w, w_kv_t, cos_t, sin_t, *(() if rows is None else rows))


def _sb_prompt_kernel(q_ref, k_ref, v_ref, o_ref, kb_ref, vb_ref, acc_ref, *, tq):
    qi = pl.program_id(2)
    lane_q = lax.broadcasted_iota(jnp.int32, (tq, LANES), 1)

    @pl.when(qi == 0)
    def _():
        sub = lax.broadcasted_iota(jnp.int32, (LANES, 1), 0)
        v = v_ref[0, 0]
        kb_ref[...] = k_ref[0, 0].astype(BF16)
        vb_ref[0] = jnp.where(sub < HEAD_DIM, v, 0.0).astype(BF16)
        vb_ref[1] = jnp.where(sub >= HEAD_DIM, v, 0.0).astype(BF16)

    q = q_ref[...] * (1.0 / math.sqrt(HEAD_DIM))
    q_heads = (jnp.where(lane_q < HEAD_DIM, q, 0.0).astype(BF16),
               jnp.where(lane_q >= HEAD_DIM, q, 0.0).astype(BF16))
    row = lax.broadcasted_iota(jnp.int32, (tq, tq), 0)
    col = lax.broadcasted_iota(jnp.int32, (tq, tq), 1)
    later = jnp.where(row > col, 1.0, 0.0).astype(BF16)
    strictly_causal = col < row
    acc_ref[...] = jnp.zeros_like(acc_ref)

    def block(kb, carry, diagonal):
        start = pl.multiple_of(kb * tq, tq)
        k = kb_ref[:, pl.ds(start, tq)]
        new_carry = []
        contrib = None
        for hd in range(2):
            z = _dg(q_heads[hd], k, _NN)
            sp = _softplus(z)
            if diagonal:
                sp = jnp.where(strictly_causal, sp, 0.0)
            after = _mm_exact_rhs(sp, later)
            total = after[:, 0:1] + sp[:, 0:1]
            w = jnp.exp(z - sp - after - carry[hd])
            if diagonal:
                w = jnp.where(strictly_causal, w, 0.0)
            pv = _dg(w.astype(BF16), vb_ref[hd, :, pl.ds(start, tq)], _NT)
            contrib = pv if contrib is None else contrib + pv
            new_carry.append(carry[hd] + total)
        acc_ref[...] += contrib
        return tuple(new_carry)

    zero = jnp.zeros((tq, 1), F32)
    c0, c1 = block(qi, (zero, zero), True)

    def more(state):
        it, a, b = state
        return jnp.logical_and(it < qi, jnp.min(jnp.minimum(a, b)) < SB_UNDERFLOW)

    def step(state):
        it, a, b = state
        a, b = block(qi - 1 - it, (a, b), False)
        return it + 1, a, b

    lax.while_loop(more, step, (jnp.int32(0), c0, c1))
    o_ref[...] = acc_ref[...].astype(o_ref.dtype)


def _sb_prompt(proj, k_t, v_t, layer, *, batch, seq, tq):
    nq = seq // tq
    t = batch * seq
    cq = C_SBQ // LANES
    kv_spec = pl.BlockSpec((1, 1, LANES, seq), lambda b, hp, qi: (layer, b, hp, 0))
    return pl.pallas_call(
        functools.partial(_sb_prompt_kernel, tq=tq),
        grid=(batch, W_BRANCH // LANES, nq),
        in_specs=[pl.BlockSpec((tq, LANES), lambda b, hp, qi: (b * nq + qi, cq + hp)), kv_spec, kv_spec],
        out_specs=pl.BlockSpec((tq, LANES), lambda b, hp, qi: (b * nq + qi, hp)),
        out_shape=jax.ShapeDtypeStruct((t, W_BRANCH), BF16),
        scratch_shapes=[pltpu.VMEM((LANES, seq), BF16), pltpu.VMEM((2, LANES, seq), BF16),
                        pltpu.VMEM((tq, LANES), F32)],
        compiler_params=_cparams(("parallel", "parallel", "arbitrary")),
    )(proj, k_t, v_t)


def _lambda_value(lam_ref, lam_init):
    lv = lam_ref[...]
    s1 = jnp.sum(lv[0:1, :] * lv[1:2, :], axis=-1, keepdims=True)
    s2 = jnp.sum(lv[2:3, :] * lv[3:4, :], axis=-1, keepdims=True)
    return jnp.exp(s1) - jnp.exp(s2) + lam_init


def _diff_prompt_kernel(q_ref, k_ref, v_ref, lam_ref, nw_ref, o_ref, kb_ref, vb_ref, acc_ref, m_ref, l_ref,
                        *, tq, big, lam_init):
    qi = pl.program_id(2)

    @pl.when(qi == 0)
    def _():
        kb_ref[...] = k_ref[0].astype(BF16)
        vb_ref[...] = v_ref[0].astype(BF16)

    q = q_ref[...] * (LOG2E / math.sqrt(HEAD_DIM))
    lane_q = lax.broadcasted_iota(jnp.int32, (tq, LANES), 1)
    q_comp = (jnp.where(lane_q < HEAD_DIM, q, 0.0).astype(BF16),
              jnp.where(lane_q >= HEAD_DIM, q, 0.0).astype(BF16))
    row = lax.broadcasted_iota(jnp.int32, (tq, tq), 0)
    col = lax.broadcasted_iota(jnp.int32, (tq, tq), 1)
    causal = col <= row

    def block(start, size, diagonal, first):
        start = pl.multiple_of(start, tq)
        k = kb_ref[pl.ds(start, size), :]
        v = vb_ref[pl.ds(start, size), :]
        for c in range(2):
            s = _dg(q_comp[c], k, _NT)
            if diagonal:
                s = jnp.where(causal, s, -jnp.inf)
            blk_max = jnp.max(s, axis=-1, keepdims=True)
            if first:
                m_new = jnp.broadcast_to(blk_max, (tq, LANES))
            else:
                m_old = m_ref[c]
                m_new = jnp.maximum(m_old, blk_max)
                alpha = jnp.exp2(m_old - m_new)
            p = jnp.exp2(s - jnp.concatenate([m_new] * (size // LANES), axis=1))
            p_sum = jnp.sum(p, axis=-1, keepdims=True)
            pv = _dg(p.astype(BF16), v, _NN)
            if first:
                l_ref[c] = jnp.broadcast_to(p_sum, (tq, LANES))
                acc_ref[c] = pv
            else:
                l_ref[c] = alpha * l_ref[c] + p_sum
                acc_ref[c] = alpha * acc_ref[c] + pv
            m_ref[c] = m_new

    block(qi * tq, tq, True, True)
    n_big = (qi * tq) // big
    n_small = (qi * tq - n_big * big) // tq

    def big_body(it, _):
        block(it * big, big, False, False)
        return 0

    def small_body(it, _):
        block(n_big * big + it * tq, tq, False, False)
        return 0

    lax.fori_loop(0, n_big, big_body, 0)
    lax.fori_loop(0, n_small, small_body, 0)
    lam = _lambda_value(lam_ref, lam_init)
    o = acc_ref[0] / l_ref[0] - lam * (acc_ref[1] / l_ref[1])
    o_ref[...] = (_rms(o, nw_ref[...]) * (1.0 - lam_init)).astype(o_ref.dtype)


def _diff_prompt(proj, k_rows, v_rows, layer, lam_vecs, diff_norm_w, *, batch, seq, tq, lam_init):
    nq = seq // tq
    t = batch * seq
    cq = C_DQ // LANES
    big = min(DIFF_KEY_CHUNK, seq)
    kv_spec = pl.BlockSpec((1, seq, LANES), lambda b, hh, qi: (layer, b, hh))
    return pl.pallas_call(
        functools.partial(_diff_prompt_kernel, tq=tq, big=big, lam_init=lam_init),
        grid=(batch, W_BRANCH // LANES, nq),
        in_specs=[
            pl.BlockSpec((tq, LANES), lambda b, hh, qi: (b * nq + qi, cq + hh)),
            kv_spec, kv_spec,
            pl.BlockSpec((4, HEAD_DIM), lambda b, hh, qi: (0, 0)),
            pl.BlockSpec((1, LANES), lambda b, hh, qi: (0, 0)),
        ],
        out_specs=pl.BlockSpec((tq, LANES), lambda b, hh, qi: (b * nq + qi, hh)),
        out_shape=jax.ShapeDtypeStruct((t, W_BRANCH), BF16),
        scratch_shapes=[pltpu.VMEM((seq, LANES), BF16), pltpu.VMEM((seq, LANES), BF16),
                        pltpu.VMEM((2, tq, LANES), F32), pltpu.VMEM((2, tq, LANES), F32),
                        pltpu.VMEM((2, tq, LANES), F32)],
        compiler_params=_cparams(("parallel", "parallel", "arbitrary")),
    )(proj, k_rows, v_rows, lam_vecs, diff_norm_w)


def _ssd_prompt_kernel(z_ref, xa_ref, xb_ref, cw_ref, cb_ref, dtb_ref, alog_ref, dsk_ref, nw_ref,
                       y_ref, hout_ref, cout_ref, xpad_ref, h_ref, *, n_heads, n_groups):
    c = pl.program_id(1)
    nc = pl.num_programs(1)
    L = SSM_CHUNK
    P = HEAD_DIM
    N = SSM_STATE
    xw = n_heads * P
    gn = n_groups * N

    @pl.when(c == 0)
    def _():
        xpad_ref[0:8, :] = jnp.zeros((8, xpad_ref.shape[1]), F32)
        h_ref[...] = jnp.zeros_like(h_ref)

    xb = xb_ref[...]
    xbc = jnp.concatenate([xa_ref[...], xb[:, :2 * gn]], axis=1)
    xpad_ref[8:8 + L, :] = xbc
    conv = cb_ref[...]
    for tap in range(CONV_TAPS):
        conv = conv + xpad_ref[pl.ds(8 - (CONV_TAPS - 1) + tap, L), :] * cw_ref[tap:tap + 1, :]
    conv = conv * jax.nn.sigmoid(conv)
    xpad_ref[0:8, :] = xbc[L - 8:, :]

    @pl.when(c == nc - 1)
    def _():
        cout_ref[0] = xbc[L - (CONV_TAPS - 1):, :]

    xs = conv[:, :xw]
    bmat = conv[:, xw:xw + gn]
    cmat = conv[:, xw + gn:]

    lane = lax.broadcasted_iota(jnp.int32, (1, LANES), 1)
    head_lane = lane < n_heads
    dt = jnp.where(head_lane, _softplus(xb[:, 2 * gn:2 * gn + LANES] + dtb_ref[...]), 0.0)
    a = jnp.where(head_lane, -jnp.exp(alog_ref[...]), 0.0)
    dta = dt * a
    ri = lax.broadcasted_iota(jnp.int32, (L, L), 0)
    ci = lax.broadcasted_iota(jnp.int32, (L, L), 1)
    lower = jnp.where(ri >= ci, 1.0, 0.0).astype(BF16)
    dta_h, dta_l = _hi_lo(dta)
    dta_l2 = (dta - dta_h.astype(F32) - dta_l.astype(F32)).astype(BF16)
    cum = _dg(lower, dta_h, _NN) + _dg(lower, dta_l, _NN) + _dg(lower, dta_l2, _NN)
    cum_t = cum.T
    dt_t = dt.T
    causal = ri >= ci

    y_parts = []
    for g in range(n_groups):
        b_g = bmat[:, g * N:(g + 1) * N]
        c_g = cmat[:, g * N:(g + 1) * N]
        cb = _mm(c_g, b_g, False, _NT)
        for hh in range(n_heads // n_groups):
            hd = g * (n_heads // n_groups) + hh
            x_h = xs[:, hd * P:(hd + 1) * P]
            cum_col = cum[:, hd:hd + 1]
            cum_row = cum_t[hd:hd + 1, :]
            dt_col = dt[:, hd:hd + 1]
            dt_row = dt_t[hd:hd + 1, :]
            decay = jnp.exp(jnp.where(causal, cum_col - cum_row, -jnp.inf))
            y_intra = _mm(cb * decay * dt_row, x_h, False)
            h_prev = h_ref[hd]
            y_inter = _mm(c_g, h_prev, False, _NT) * jnp.exp(cum_col)
            cum_end = cum[L - 1:L, hd:hd + 1]
            w_end = jnp.exp(cum_end - cum_col) * dt_col
            s_chunk = _mm(x_h * w_end, b_g, False, _TN)
            h_ref[hd] = jnp.exp(cum_end) * h_prev + s_chunk
            y_parts.append(y_intra + y_inter + dsk_ref[0:1, hd:hd + 1] * x_h)
    y = jnp.concatenate(y_parts, axis=1)
    zg = z_ref[...]
    y = y * (zg * jax.nn.sigmoid(zg))
    gw = xw // n_groups
    nw = nw_ref[...]
    y = jnp.concatenate([_rms(y[:, g * gw:(g + 1) * gw], nw[:, g * gw:(g + 1) * gw])
                         for g in range(n_groups)], axis=1)
    y_ref[...] = y.astype(y_ref.dtype)

    @pl.when(c == nc - 1)
    def _():
        hout_ref[0] = h_ref[...]


def _ssd_prompt(proj, conv_w, conv_b, dt_bias, a_log, d_skip, norm_w, *, batch, seq, n_heads, n_groups):
    L = SSM_CHUNK
    nc = seq // L
    t = batch * seq
    cdim = conv_w.shape[1]
    cz, cx = C_Z // COL_TILE, C_XBC // COL_TILE
    full = lambda shape: pl.BlockSpec(shape, lambda b, c: (0,) * len(shape))
    return pl.pallas_call(
        functools.partial(_ssd_prompt_kernel, n_heads=n_heads, n_groups=n_groups),
        grid=(batch, nc),
        in_specs=[
            pl.BlockSpec((L, COL_TILE), lambda b, c: (b * nc + c, cz)),
            pl.BlockSpec((L, COL_TILE), lambda b, c: (b * nc + c, cx)),
            pl.BlockSpec((L, COL_TILE), lambda b, c: (b * nc + c, cx + 1)),
            full((CONV_TAPS, cdim)), full((1, cdim)), full((1, LANES)), full((1, LANES)),
            full((1, LANES)), full((1, W_BRANCH)),
        ],
        out_specs=[
            pl.BlockSpec((L, W_BRANCH), lambda b, c: (b * nc + c, 0)),
            pl.BlockSpec((1, n_heads, HEAD_DIM, SSM_STATE), lambda b, c: (b, 0, 0, 0)),
            pl.BlockSpec((1, CONV_TAPS - 1, cdim), lambda b, c: (b, 0, 0)),
        ],
        out_shape=[
            jax.ShapeDtypeStruct((t, W_BRANCH), BF16),
            jax.ShapeDtypeStruct((batch, n_heads, HEAD_DIM, SSM_STATE), F32),
            jax.ShapeDtypeStruct((batch, CONV_TAPS - 1, cdim), F32),
        ],
        scratch_shapes=[pltpu.VMEM((8 + L, cdim), F32), pltpu.VMEM((n_heads, HEAD_DIM, SSM_STATE), F32)],
        compiler_params=_cparams(("parallel", "arbitrary")),
    )(proj, proj, proj, conv_w, conv_b, dt_bias, a_log, d_skip, norm_w)


def _merge_kernel(x_ref, g0_ref, g1_ref, g2_ref, a_ref, b_ref, c_ref, wa_ref, wb_ref, wc_ref, wo_ref,
                  o_ref, *, precise):
    m = (jax.nn.sigmoid(g0_ref[...]) * _mm(a_ref[...], wa_ref[...], precise)
         + jax.nn.sigmoid(g1_ref[...]) * _mm(b_ref[...], wb_ref[...], precise)
         + jax.nn.sigmoid(g2_ref[...]) * _mm(c_ref[...], wc_ref[...], precise))
    o_ref[...] = x_ref[...] + _mm(m, wo_ref[...], precise)


def _merge(x, proj, o_sb, y_ssm, o_diff, w_sb, w_ssm, w_diff, w_o, *, precise, tm):
    t = x.shape[0]
    full = lambda shape: pl.BlockSpec(shape, lambda i: (0, 0))
    gate = lambda n: pl.BlockSpec((tm, D_MODEL), lambda i: (i, C_GATE // D_MODEL + n))
    branch = pl.BlockSpec((tm, W_BRANCH), lambda i: (i, 0))
    return pl.pallas_call(
        functools.partial(_merge_kernel, precise=precise),
        grid=(t // tm,),
        in_specs=[pl.BlockSpec((tm, D_MODEL), lambda i: (i, 0)), gate(0), gate(1), gate(2),
                  branch, branch, branch,
                  full((W_BRANCH, D_MODEL)), full((W_BRANCH, D_MODEL)), full((W_BRANCH, D_MODEL)),
                  full((D_MODEL, D_MODEL))],
        out_specs=pl.BlockSpec((tm, D_MODEL), lambda i: (i, 0)),
        out_shape=jax.ShapeDtypeStruct((t, D_MODEL), F32),
        compiler_params=_cparams(("parallel",)),
    )(x, proj, proj, proj, o_sb, y_ssm, o_diff, w_sb, w_ssm, w_diff, w_o)


def _route(logits):
    lane = lax.broadcasted_iota(jnp.int32, logits.shape, 1)
    neg = -jnp.inf
    big = jnp.int32(1 << 20)
    g = jnp.where(lane < MOE_GROUPS, logits, neg)
    g_max = jnp.max(g, axis=-1, keepdims=True)
    g_sel = jnp.min(jnp.where(g == g_max, lane, big), axis=-1, keepdims=True)
    g_w = 1.0 / jnp.sum(jnp.exp(g - g_max), axis=-1, keepdims=True)
    lo = MOE_GROUPS + g_sel * MOE_PER_GROUP
    e = jnp.where((lane >= lo) & (lane < lo + MOE_PER_GROUP), logits, neg)
    v1 = jnp.max(e, axis=-1, keepdims=True)
    i1 = jnp.min(jnp.where(e == v1, lane, big), axis=-1, keepdims=True)
    e2 = jnp.where(lane == i1, neg, e)
    v2 = jnp.max(e2, axis=-1, keepdims=True)
    i2 = jnp.min(jnp.where(e2 == v2, lane, big), axis=-1, keepdims=True)
    r = jnp.exp(v2 - v1)
    w1 = g_w / (1.0 + r)
    w2 = g_w * r / (1.0 + r)
    return jnp.where(lane == i1, w1, 0.0) + jnp.where(lane == i2, w2, 0.0)


def _moe_kernel(x_ref, nw_ref, wr_ref, wg_ref, wu_ref, wd_ref, fw_ref, o_ref,
                hh_ref, hl_ref, comb_ref, acc_ref, *, precise, final):
    e = pl.program_id(1)
    ne = pl.num_programs(1)

    @pl.when(e == 0)
    def _():
        h = _rms(x_ref[...], nw_ref[...])
        hi = h.astype(BF16)
        lo = (h - hi.astype(F32)).astype(BF16)
        hh_ref[...] = hi
        hl_ref[...] = lo
        wrh, wrl = _hi_lo(wr_ref[...])
        logits = _dg(hi, wrh, _NN) + _dg(hi, wrl, _NN) + _dg(lo, wrh, _NN)
        comb_ref[...] = _route(logits)
        acc_ref[...] = jnp.zeros_like(acc_ref)

    hh = hh_ref[...]
    if precise:
        hl = hl_ref[...]
        gh, gl = _hi_lo(wg_ref[0, 0])
        uh, ul = _hi_lo(wu_ref[0, 0])
        gate = _dg(hh, gh, _NN) + _dg(hh, gl, _NN) + _dg(hl, gh, _NN)
        up = _dg(hh, uh, _NN) + _dg(hh, ul, _NN) + _dg(hl, uh, _NN)
    else:
        gate = _dg(hh, wg_ref[0, 0], _NN)
        up = _dg(hh, wu_ref[0, 0], _NN)
    lane = lax.broadcasted_iota(jnp.int32, comb_ref.shape, 1)
    cw = jnp.sum(jnp.where(lane == e + MOE_GROUPS, comb_ref[...], 0.0), axis=-1, keepdims=True)
    he = gate * jax.nn.sigmoid(gate) * up * cw
    acc_ref[...] += _mm(he, wd_ref[0, 0], precise)

    @pl.when(e == ne - 1)
    def _():
        y = x_ref[...] + acc_ref[...]
        if final:
            y = _rms(y, fw_ref[...])
        o_ref[...] = y


def _moe(x, norm_w, w_route, w_gate, w_up, w_down, final_w, layer, *, precise, final, tm):
    t = x.shape[0]
    _, n_exp, _, ff = w_gate.shape
    return pl.pallas_call(
        functools.partial(_moe_kernel, precise=precise, final=final),
        grid=(t // tm, n_exp),
        in_specs=[
            pl.BlockSpec((tm, D_MODEL), lambda i, e: (i, 0)),
            pl.BlockSpec((1, D_MODEL), lambda i, e: (0, 0)),
            pl.BlockSpec((D_MODEL, LANES), lambda i, e: (0, 0)),
            pl.BlockSpec((1, 1, D_MODEL, ff), lambda i, e: (layer, e, 0, 0)),
            pl.BlockSpec((1, 1, D_MODEL, ff), lambda i, e: (layer, e, 0, 0)),
            pl.BlockSpec((1, 1, ff, D_MODEL), lambda i, e: (layer, e, 0, 0)),
            pl.BlockSpec((1, D_MODEL), lambda i, e: (0, 0)),
        ],
        out_specs=pl.BlockSpec((tm, D_MODEL), lambda i, e: (i, 0)),
        out_shape=jax.ShapeDtypeStruct((t, D_MODEL), F32),
        scratch_shapes=[pltpu.VMEM((tm, D_MODEL), BF16), pltpu.VMEM((tm, D_MODEL), BF16),
                        pltpu.VMEM((tm, LANES), F32), pltpu.VMEM((tm, D_MODEL), F32)],
        compiler_params=_cparams(("parallel", "arbitrary")),
    )(x, norm_w, w_route, w_gate, w_up, w_down, final_w)


def _row_to_col(row, eye):
    return jnp.sum(jnp.where(eye, jnp.broadcast_to(row, eye.shape), 0.0), axis=1, keepdims=True)


def _col_to_row(col, eye):
    return jnp.sum(jnp.where(eye, jnp.broadcast_to(col, eye.shape), 0.0), axis=0, keepdims=True)


def _sb_sample_kernel(*refs, pages_per_step, n_heads, resume):
    if resume:
        pt_ref, more_ref, q_ref, prev_o_ref, prev_c_ref = refs[:5]
        refs = refs[5:]
    else:
        pt_ref, q_ref = refs[:2]
        refs = refs[2:]
    k_refs = refs[:pages_per_step]
    v_refs = refs[pages_per_step:2 * pages_per_step]
    if resume:
        o_ref, qb_ref, acc_ref, c_ref = refs[2 * pages_per_step:]
    else:
        o_ref, c_out_ref, more_out_ref, qb_ref, acc_ref, c_ref = refs[2 * pages_per_step:]
    b = pl.program_id(0)
    j = pl.program_id(1)
    nj = pl.num_programs(1)
    P = HEAD_DIM
    page = qb_ref.shape[-1]
    eye = lax.broadcasted_iota(jnp.int32, (P, P), 0) == lax.broadcasted_iota(jnp.int32, (P, P), 1)

    @pl.when(j == 0)
    def _():
        q = q_ref[0] * (1.0 / math.sqrt(HEAD_DIM))
        for hd in range(n_heads):
            qb_ref[hd] = jnp.broadcast_to(_row_to_col(q[:, hd * P:(hd + 1) * P], eye), (P, page))
        acc_ref[...] = jnp.zeros_like(acc_ref)
        c_ref[...] = prev_c_ref[0] if resume else jnp.zeros_like(c_ref)

    def walk():
        ri = lax.broadcasted_iota(jnp.int32, (page, page), 0)
        ci = lax.broadcasted_iota(jnp.int32, (page, page), 1)
        later = jnp.where(ri > ci, 1.0, 0.0).astype(BF16)
        carry = c_ref[:, 0:1]
        for p in range(pages_per_step):
            z = jnp.concatenate([jnp.sum(k_refs[p][0, 0, hd] * qb_ref[hd], axis=0, keepdims=True)
                                 for hd in range(n_heads)], axis=0)
            sp = _softplus(z)
            sph, spl = _hi_lo(sp)
            spl2 = (sp - sph.astype(F32) - spl.astype(F32)).astype(BF16)
            after = _dg(sph, later, _NN) + _dg(spl, later, _NN) + _dg(spl2, later, _NN)
            w = jnp.exp(z - sp - after - carry)
            for hd in range(n_heads):
                acc_ref[hd] += v_refs[p][0, 0, hd] * w[hd:hd + 1, :]
            carry = carry + after[:, 0:1] + sp[:, 0:1]
        c_ref[...] = jnp.broadcast_to(carry, c_ref.shape)

    if resume:
        pl.when(more_ref[b] != 0)(walk)
    else:
        walk()

    @pl.when(j == nj - 1)
    def _():
        o = jnp.concatenate(
            [_col_to_row(jnp.sum(acc_ref[hd], axis=1, keepdims=True), eye) for hd in range(n_heads)], axis=1)
        if resume:
            o_ref[0] = o + prev_o_ref[0]
        else:
            o_ref[0] = o
            carry = c_ref[...]
            c_out_ref[0] = carry
            more_out_ref[0] = jnp.broadcast_to(jnp.where(jnp.min(carry) < SB_UNDERFLOW, 1, 0).astype(jnp.int32),
                                               (1, LANES))


def _sb_sample_walk(page_table, q, cache_k_t, cache_v_t, layer, newest, n_steps, resume_args, *, pages_per_step):
    nb, n_pages = page_table.shape
    _, _, n_heads, hdim, page = cache_k_t.shape
    width = n_heads * hdim
    pt_flat = page_table.reshape(-1)
    resume = resume_args is not None

    def page_spec(p):
        def index(b, j, pt, *more):
            phys = pt[b * n_pages + newest - j * pages_per_step - p]
            if resume:
                phys = jnp.where(more[0][b] != 0, phys, 0)
            return (layer, phys, 0, 0, 0)
        return pl.BlockSpec((1, 1, n_heads, hdim, page), index)

    specs = [page_spec(p) for p in range(pages_per_step)]
    row = pl.BlockSpec((1, 1, width), lambda b, j, *_: (b, 0, 0))
    carry_spec = pl.BlockSpec((1, n_heads, LANES), lambda b, j, *_: (b, 0, 0))
    flag_spec = pl.BlockSpec((1, 1, LANES), lambda b, j, *_: (b, 0, 0))
    scratch = [pltpu.VMEM((n_heads, hdim, page), F32), pltpu.VMEM((n_heads, hdim, page), F32),
               pltpu.VMEM((n_heads, LANES), F32)]
    kernel_fn = functools.partial(_sb_sample_kernel, pages_per_step=pages_per_step, n_heads=n_heads, resume=resume)
    pages = [cache_k_t] * pages_per_step + [cache_v_t] * pages_per_step
    if resume:
        more, prev_o, prev_c = resume_args
        grid_spec = pltpu.PrefetchScalarGridSpec(
            num_scalar_prefetch=2, grid=(nb, n_steps),
            in_specs=[row, row, carry_spec] + specs + specs, out_specs=row, scratch_shapes=scratch)
        return pl.pallas_call(
            kernel_fn, grid_spec=grid_spec, out_shape=jax.ShapeDtypeStruct((nb, 1, width), F32),
            compiler_params=_cparams(("parallel", "arbitrary")),
        )(pt_flat, more, q, prev_o, prev_c, *pages)
    grid_spec = pltpu.PrefetchScalarGridSpec(
        num_scalar_prefetch=1, grid=(nb, n_steps),
        in_specs=[row] + specs + specs, out_specs=[row, carry_spec, flag_spec], scratch_shapes=scratch)
    return pl.pallas_call(
        kernel_fn, grid_spec=grid_spec,
        out_shape=[jax.ShapeDtypeStruct((nb, 1, width), F32), jax.ShapeDtypeStruct((nb, n_heads, LANES), F32),
                   jax.ShapeDtypeStruct((nb, 1, LANES), jnp.int32)],
        compiler_params=_cparams(("parallel", "arbitrary")),
    )(pt_flat, q, *pages)


def _sb_sample(page_table, q, cache_k_t, cache_v_t, layer, *, pages_per_step):
    n_pages = page_table.shape[1]
    o, carry, more = _sb_sample_walk(page_table, q, cache_k_t, cache_v_t, layer, n_pages - 1, 1, None,
                                     pages_per_step=pages_per_step)
    n_steps = n_pages // pages_per_step - 1
    if n_steps == 0:
        return o
    return _sb_sample_walk(page_table, q, cache_k_t, cache_v_t, layer, n_pages - 1 - pages_per_step, n_steps,
                           (more[:, 0, 0], o, carry), pages_per_step=pages_per_step)


def _diff_sample_kernel(pt_ref, q_ref, kn_ref, vn_ref, lam_ref, nw_ref, *refs, pages_per_step, n_heads, lam_init):
    k_refs = refs[:pages_per_step]
    v_refs = refs[pages_per_step:2 * pages_per_step]
    o_ref, q8_ref, acc_ref, m_ref, l_ref = refs[2 * pages_per_step:]
    j = pl.program_id(1)
    nj = pl.num_programs(1)
    rows = k_refs[0].shape[2]
    comp_of_lane = lax.broadcasted_iota(jnp.int32, (LANES, LANES), 0) // HEAD_DIM
    sel = [jnp.where(comp_of_lane == c, 1.0, 0.0).astype(BF16) for c in range(2)]
    row8 = lax.broadcasted_iota(jnp.int32, (8, LANES), 0)

    def per_sublane(vec):
        return jnp.concatenate([vec[:, (i % n_heads) * LANES:(i % n_heads + 1) * LANES] for i in range(8)], axis=0)

    @pl.when(j == 0)
    def _():
        q8 = per_sublane(q_ref[0]) * (LOG2E / math.sqrt(HEAD_DIM))
        q8_ref[...] = q8
        prod = per_sublane(kn_ref[0]) * q8
        v_new = jnp.where(row8 < n_heads, per_sublane(vn_ref[0]), 0.0)
        for c in range(2):
            m_ref[c] = _mm_exact_rhs(prod, sel[c])
            l_ref[c] = jnp.where(row8 < n_heads, 1.0, 0.0)
            acc_ref[c] = v_new

    q8 = q8_ref[...]
    sel_both = jnp.concatenate([jnp.concatenate(sel, axis=1)] * 2, axis=0)
    for p in range(pages_per_step):
        k3 = k_refs[p][0, 0].reshape(rows // 8, 8, LANES)
        v3 = v_refs[p][0, 0].reshape(rows // 8, 8, LANES)
        prod_hi, prod_lo = _hi_lo((k3 * q8[None]).reshape(rows, LANES))
        s_both = _dg(jnp.concatenate([prod_hi, prod_lo], axis=1), sel_both, _NN)
        for c in range(2):
            s = s_both[:, c * LANES:(c + 1) * LANES].reshape(rows // 8, 8, LANES)
            blk_max = jnp.max(s, axis=0)
            blk_max = jnp.maximum(blk_max, pltpu.roll(blk_max, n_heads, 0))
            m_old = m_ref[c]
            m_new = jnp.maximum(m_old, blk_max)
            alpha = jnp.exp2(m_old - m_new)
            pr = jnp.exp2(s - m_new[None])
            l_ref[c] = alpha * l_ref[c] + jnp.sum(pr, axis=0)
            acc_ref[c] = alpha * acc_ref[c] + jnp.sum(pr * v3, axis=0)
            m_ref[c] = m_new

    @pl.when(j == nj - 1)
    def _():
        lam = _lambda_value(lam_ref, lam_init)
        outs = []
        for c in range(2):
            l = l_ref[c]
            a = acc_ref[c]
            outs.append((a + pltpu.roll(a, n_heads, 0)) / (l + pltpu.roll(l, n_heads, 0)))
        o = _rms(outs[0] - lam * outs[1], nw_ref[...]) * (1.0 - lam_init)
        o_ref[0] = jnp.concatenate([o[hd:hd + 1, :] for hd in range(n_heads)], axis=1)


def _diff_sample(page_table, q, k_new, v_new, lam_vecs, diff_norm_w, cache_k_r, cache_v_r, layer, *,
                 pages_per_step, n_heads, lam_init):
    nb, n_pages = page_table.shape
    rows = cache_k_r.shape[2]
    width = n_heads * LANES
    nj = n_pages // pages_per_step
    pt_flat = page_table.reshape(-1)

    def page_spec(p):
        def index(b, j, pt):
            return (layer, pt[b * n_pages + j * pages_per_step + p], 0, 0)
        return pl.BlockSpec((1, 1, rows, LANES), index)

    specs = [page_spec(p) for p in range(pages_per_step)]
    row = pl.BlockSpec((1, 1, width), lambda b, j, pt: (b, 0, 0))
    grid_spec = pltpu.PrefetchScalarGridSpec(
        num_scalar_prefetch=1,
        grid=(nb, nj),
        in_specs=[row, row, row,
                  pl.BlockSpec((4, HEAD_DIM), lambda b, j, pt: (0, 0)),
                  pl.BlockSpec((1, LANES), lambda b, j, pt: (0, 0))] + specs + specs,
        out_specs=row,
        scratch_shapes=[pltpu.VMEM((8, LANES), F32), pltpu.VMEM((2, 8, LANES), F32),
                        pltpu.VMEM((2, 8, LANES), F32), pltpu.VMEM((2, 8, LANES), F32)],
    )
    return pl.pallas_call(
        functools.partial(_diff_sample_kernel, pages_per_step=pages_per_step, n_heads=n_heads, lam_init=lam_init),
        grid_spec=grid_spec,
        out_shape=jax.ShapeDtypeStruct((nb, 1, width), F32),
        compiler_params=_cparams(("parallel", "arbitrary")),
    )(pt_flat, q, k_new, v_new, lam_vecs, diff_norm_w,
      *([cache_k_r] * pages_per_step), *([cache_v_r] * pages_per_step))


def _ssd_sample_kernel(z_ref, xa_ref, xb_ref, cprev_ref, hprev_ref, cw_ref, cb_ref, dtb_ref, alog_ref, dsk_ref,
                       nw_ref, y_ref, hout_ref, cout_ref, *, n_heads, n_groups):
    P = HEAD_DIM
    N = SSM_STATE
    xw = n_heads * P
    gn = n_groups * N
    xb = xb_ref[0]
    xbc = jnp.concatenate([xa_ref[0], xb[:, :2 * gn]], axis=1)
    prev = cprev_ref[0]
    conv = cb_ref[...] + xbc * cw_ref[CONV_TAPS - 1:CONV_TAPS, :]
    for tap in range(CONV_TAPS - 1):
        conv = conv + prev[tap:tap + 1, :] * cw_ref[tap:tap + 1, :]
    conv = conv * jax.nn.sigmoid(conv)
    cout_ref[0, 0:CONV_TAPS - 2, :] = prev[1:, :]
    cout_ref[0, CONV_TAPS - 2:CONV_TAPS - 1, :] = xbc
    xs = conv[:, :xw]
    dt = _softplus(xb[:, 2 * gn:2 * gn + LANES] + dtb_ref[...])
    a = -jnp.exp(alog_ref[...])
    decay = jnp.exp(dt * a)
    eye = lax.broadcasted_iota(jnp.int32, (P, P), 0) == lax.broadcasted_iota(jnp.int32, (P, P), 1)
    y_parts = []
    for hd in range(n_heads):
        g = hd // (n_heads // n_groups)
        b_g = conv[:, xw + g * N:xw + (g + 1) * N]
        c_g = conv[:, xw + gn + g * N:xw + gn + (g + 1) * N]
        x_h = xs[:, hd * P:(hd + 1) * P]
        x_col = jnp.sum(jnp.where(eye, jnp.broadcast_to(x_h, (P, P)), 0.0), axis=1, keepdims=True)
        h_new = decay[:, hd:hd + 1] * hprev_ref[0, hd] + (dt[:, hd:hd + 1] * x_col) * b_g
        hout_ref[0, hd] = h_new
        y_col = jnp.sum(h_new * c_g, axis=1, keepdims=True)
        y_row = jnp.sum(jnp.where(eye, jnp.broadcast_to(y_col, (P, P)), 0.0), axis=0, keepdims=True)
        y_parts.append(y_row + dsk_ref[0:1, hd:hd + 1] * x_h)
    y = jnp.concatenate(y_parts, axis=1)
    zg = z_ref[0]
    y = y * (zg * jax.nn.sigmoid(zg))
    gw = xw // n_groups
    nw = nw_ref[...]
    y_ref[0] = jnp.concatenate([_rms(y[:, g * gw:(g + 1) * gw], nw[:, g * gw:(g + 1) * gw])
                                for g in range(n_groups)], axis=1)


def _ssd_sample(proj3, conv_prev, ssm_prev, conv_w, conv_b, dt_bias, a_log, d_skip, norm_w, *, n_heads, n_groups):
    nb = proj3.shape[0]
    cdim = conv_w.shape[1]
    cz, cx = C_Z // COL_TILE, C_XBC // COL_TILE
    full = lambda shape: pl.BlockSpec(shape, lambda b: (0,) * len(shape))
    col = lambda cidx: pl.BlockSpec((1, 1, COL_TILE), lambda b: (b, 0, cidx))
    return pl.pallas_call(
        functools.partial(_ssd_sample_kernel, n_heads=n_heads, n_groups=n_groups),
        grid=(nb,),
        in_specs=[col(cz), col(cx), col(cx + 1),
                  pl.BlockSpec((1, CONV_TAPS - 1, cdim), lambda b: (b, 0, 0)),
                  pl.BlockSpec((1, n_heads, HEAD_DIM, SSM_STATE), lambda b: (b, 0, 0, 0)),
                  full((CONV_TAPS, cdim)), full((1, cdim)), full((1, LANES)), full((1, LANES)),
                  full((1, LANES)), full((1, W_BRANCH))],
        out_specs=[pl.BlockSpec((1, 1, W_BRANCH), lambda b: (b, 0, 0)),
                   pl.BlockSpec((1, n_heads, HEAD_DIM, SSM_STATE), lambda b: (b, 0, 0, 0)),
                   pl.BlockSpec((1, CONV_TAPS - 1, cdim), lambda b: (b, 0, 0))],
        out_shape=[jax.ShapeDtypeStruct((nb, 1, W_BRANCH), F32),
                   jax.ShapeDtypeStruct((nb, n_heads, HEAD_DIM, SSM_STATE), F32),
                   jax.ShapeDtypeStruct((nb, CONV_TAPS - 1, cdim), F32)],
        compiler_params=_cparams(("parallel",)),
    )(proj3, proj3, proj3, conv_prev, ssm_prev, conv_w, conv_b, dt_bias, a_log, d_skip, norm_w)


def _rope_tables(pos):
    half = HEAD_DIM // 2
    inv_freq = ROPE_THETA ** (-jnp.arange(half, dtype=F32) / half)
    ang = pos.astype(F32)[:, None] * inv_freq[None, :]
    cos, sin = jnp.cos(ang), jnp.sin(ang)
    reps = LANES // HEAD_DIM
    cos_t = jnp.tile(jnp.concatenate([cos, cos], axis=1), (1, reps))
    sin_t = jnp.tile(jnp.concatenate([-sin, sin], axis=1), (1, reps))
    return cos_t, sin_t


def _rearrange_w_in(w):
    n_gate = 3 * D_MODEL
    wb = W_BRANCH
    ssm = w[:, 3 * wb:3 * wb + (C_DT - C_Z) + 8]
    d0 = 3 * wb + ssm.shape[1]
    pad = jnp.zeros((w.shape[0], C_DQ - C_Z - ssm.shape[1]), w.dtype)
    return jnp.concatenate([w[:, w.shape[1] - n_gate:], w[:, :wb], ssm, pad, w[:, d0:d0 + wb],
                            w[:, wb:3 * wb], w[:, d0 + wb:d0 + 3 * wb]], axis=1)


def _pad_lanes(v):
    v = v.reshape(1, -1)
    return jnp.pad(v, ((0, 0), (0, LANES - v.shape[1])))


def kernel(x_prompt, x_sample, cache_sb_k, cache_sb_v, cache_diff_k, cache_diff_v, state_ssm, state_conv,
           page_table, norm1_w, w_in, conv_w, conv_b, dt_bias, a_log, d_skip, ssm_norm_w,
           lam_q1, lam_k1, lam_q2, lam_k2, diff_norm_w, w_sb_out, w_ssm_out, w_diff_out, w_o,
           norm2_w, w_group, w_expert, w_gate, w_up, w_down, final_norm_w):
    bp, seq, d = x_prompt.shape
    bs = x_sample.shape[0]
    depth = w_in.shape[0]
    n_pages = page_table.shape[1]
    page = cache_sb_k.shape[2]
    sb_heads, diff_heads = cache_sb_k.shape[3], cache_diff_k.shape[3]
    n_heads = state_ssm.shape[2]
    cdim = state_conv.shape[3]
    n_groups = (cdim - n_heads * HEAD_DIM) // (2 * SSM_STATE)
    tp = bp * seq
    tq = min(256, seq)
    tq_diff = min(512, seq)
    w_gate_b, w_up_b, w_down_b = w_gate.astype(BF16), w_up.astype(BF16), w_down.astype(BF16)
    tm_proj = min(1024, seq)
    tm_merge = min(512, seq)
    tm_moe = min(1024, seq)
    pages_per_step = math.gcd(n_pages, 8)

    sb_k_t = jnp.transpose(cache_sb_k, (0, 1, 3, 4, 2))
    sb_v_t = jnp.transpose(cache_sb_v, (0, 1, 3, 4, 2))
    diff_k_r = cache_diff_k.reshape(depth, cache_diff_k.shape[1], page * diff_heads, 2 * HEAD_DIM)
    diff_v_r = cache_diff_v.reshape(depth, cache_diff_v.shape[1], page * diff_heads, 2 * HEAD_DIM)

    cos_p, sin_p = _rope_tables(jnp.arange(seq))
    cos_s, sin_s = _rope_tables(jnp.full((bs,), n_pages * page, jnp.int32))

    xp = x_prompt.reshape(tp, d)
    xs = x_sample.reshape(bs, d)
    outs = {k: [] for k in ("pssm", "pconv", "sk", "sv", "sdk", "sdv", "sssm", "sconv")}
    prompt_rows = None
    row = lambda v: v.reshape(1, -1)
    for l in range(depth):
        lam_init = 0.8 - 0.6 * math.exp(-0.3 * l)
        last = l == depth - 1
        w_in_f = _rearrange_w_in(w_in[l])
        lam_vecs = jnp.stack([lam_q1[l], lam_k1[l], lam_q2[l], lam_k2[l]])
        w_route = jnp.pad(jnp.concatenate([w_group[l], w_expert[l]], axis=1),
                          ((0, 0), (0, LANES - MOE_GROUPS - MOE_GROUPS * MOE_PER_GROUP)))
        ssd_w = (conv_w[l], row(conv_b[l]), _pad_lanes(dt_bias[l]), _pad_lanes(a_log[l]), _pad_lanes(d_skip[l]),
                 row(ssm_norm_w[l]))

        w_in_b = w_in_f.astype(BF16)
        proj, *prompt_rows = _inproj_prompt(xp, row(norm1_w[l]), w_in_b, w_in_b[:, C_SBK:C_DK].T, cos_p, sin_p,
                                            prompt_rows, l, depth=depth, batch=bp, seq=seq, tm=tm_proj)
        k_t, v_t, dk_rows, dv_rows = prompt_rows
        o_sb = _sb_prompt(proj, k_t, v_t, l, batch=bp, seq=seq, tq=tq)
        o_diff = _diff_prompt(proj, dk_rows, dv_rows, l, lam_vecs, row(diff_norm_w[l]), batch=bp, seq=seq,
                              tq=tq_diff, lam_init=lam_init)
        y_ssm, ssm_p, conv_p = _ssd_prompt(proj, *ssd_w, batch=bp, seq=seq, n_heads=n_heads, n_groups=n_groups)
        xp = _merge(xp, proj, o_sb, y_ssm, o_diff, w_sb_out[l].astype(BF16), w_ssm_out[l].astype(BF16),
                    w_diff_out[l].astype(BF16), w_o[l].astype(BF16), precise=False, tm=tm_merge)
        xp = _moe(xp, row(norm2_w[l]), w_route, w_gate_b, w_up_b, w_down_b, row(final_norm_w), l,
                  precise=False, final=last, tm=tm_moe)
        outs["pssm"].append(ssm_p)
        outs["pconv"].append(conv_p)

        proj_s = _inproj(xs, row(norm1_w[l]), w_in_f, cos_s, sin_s)
        proj3 = proj_s.reshape(bs, 1, PROJ_W)
        sec = lambda c0: proj3[:, :, c0:c0 + W_BRANCH]
        o_sb_s = _sb_sample(page_table, sec(C_SBQ), sb_k_t, sb_v_t, l, pages_per_step=pages_per_step)
        o_diff_s = _diff_sample(page_table, sec(C_DQ), sec(C_DK), sec(C_DV), lam_vecs, row(diff_norm_w[l]),
                                diff_k_r, diff_v_r, l, pages_per_step=pages_per_step, n_heads=diff_heads,
                                lam_init=lam_init)
        y_ssm_s, ssm_s, conv_s = _ssd_sample(proj3, state_conv[l], state_ssm[l], *ssd_w,
                                             n_heads=n_heads, n_groups=n_groups)
        xs = _merge(xs, proj_s, o_sb_s.reshape(bs, -1), y_ssm_s.reshape(bs, -1), o_diff_s.reshape(bs, -1),
                    w_sb_out[l], w_ssm_out[l], w_diff_out[l], w_o[l], precise=True, tm=bs)
        xs = _moe(xs, row(norm2_w[l]), w_route, w_gate, w_up, w_down, row(final_norm_w), l,
                  precise=True, final=last, tm=bs)
        outs["sk"].append(proj_s[:, C_SBK:C_SBK + W_BRANCH].reshape(bs, 1, sb_heads, HEAD_DIM))
        outs["sv"].append(proj_s[:, C_SBV:C_SBV + W_BRANCH].reshape(bs, 1, sb_heads, HEAD_DIM))
        outs["sdk"].append(proj_s[:, C_DK:C_DK + W_BRANCH].reshape(bs, 1, diff_heads, 2 * HEAD_DIM))
        outs["sdv"].append(proj_s[:, C_DV:C_DV + W_BRANCH].reshape(bs, 1, diff_heads, 2 * HEAD_DIM))
        outs["sssm"].append(ssm_s)
        outs["sconv"].append(conv_s)

    st = {k: jnp.stack(v) for k, v in outs.items()}
    k_t, v_t, dk_rows, dv_rows = prompt_rows
    heads_last = lambda a: jnp.transpose(a.reshape(depth, bp, sb_heads, HEAD_DIM, seq), (0, 1, 4, 2, 3))
    return (xp.reshape(bp, seq, d), xs.reshape(bs, 1, d),
            heads_last(k_t), heads_last(v_t),
            dk_rows.reshape(depth, bp, seq, diff_heads, 2 * HEAD_DIM),
            dv_rows.reshape(depth, bp, seq, diff_heads, 2 * HEAD_DIM), st["pssm"], st["pconv"],
            st["sk"], st["sv"], st["sdk"], st["sdv"], st["sssm"], st["sconv"])
```

```python
import functools
import math

import jax
import jax.numpy as jnp
from jax import lax
from jax.experimental import pallas as pl
from jax.experimental.pallas import tpu as pltpu

F32 = jnp.float32
BF16 = jnp.bfloat16

EPS = 1e-6
ROPE_THETA = 10000.0
LANES = 128
HEAD_DIM = 64
SSM_STATE = 64
SSM_CHUNK = 128
CONV_TAPS = 4
MOE_GROUPS = 4
MOE_PER_GROUP = 4
VMEM_LIMIT = 48 * 1024 * 1024
LOG2E = 1.4426950408889634
DIFF_KEY_CHUNK = 1024
SB_UNDERFLOW = 104.0
SB_OLDER_PAGES_PER_STEP = 24
DIFF_PAGES_PER_STEP = 16

D_MODEL = 1024
W_BRANCH = 512
C_GATE = 0
C_SBQ = 3072
C_Z = 3584
C_XBC = 4096
C_DT = 4864
C_DQ = 5120
PROMPT_W = 5632
C_SBK = 5632
C_SBV = 6144
C_DK = 6656
C_DV = 7168
PROJ_W = 7680
COL_TILE = 512


def _cparams(sem):
    return pltpu.CompilerParams(dimension_semantics=sem, vmem_limit_bytes=VMEM_LIMIT)


def _hi_lo(a):
    hi = a.astype(BF16)
    lo = (a - hi.astype(F32)).astype(BF16)
    return hi, lo


_NN = (((1,), (0,)), ((), ()))
_NT = (((1,), (1,)), ((), ()))
_TN = (((0,), (0,)), ((), ()))


def _dg(a, b, dn):
    return lax.dot_general(a, b, dn, preferred_element_type=F32)


def _mm(a, b, precise, dn=_NN):
    if not precise:
        return _dg(a.astype(BF16), b.astype(BF16), dn)
    ah, al = _hi_lo(a.astype(F32))
    bh, bl = _hi_lo(b.astype(F32))
    return _dg(ah, bh, dn) + _dg(ah, bl, dn) + _dg(al, bh, dn)


def _mm_exact_rhs(a, b_exact, dn=_NN):
    ah, al = _hi_lo(a)
    return _dg(ah, b_exact, dn) + _dg(al, b_exact, dn)


def _mm_exact_lhs(a_exact, b, dn=_NN):
    bh, bl = _hi_lo(b)
    return _dg(a_exact, bh, dn) + _dg(a_exact, bl, dn)


def _softplus(z):
    return jnp.maximum(z, 0.0) + jnp.log(1.0 + jnp.exp(-jnp.abs(z)))


def _rms(x, w):
    return x * lax.rsqrt(jnp.mean(x * x, axis=-1, keepdims=True) + EPS) * w


def _rotary(acc, cos_ref, sin_ref):
    reps = acc.shape[1] // LANES
    cos = jnp.concatenate([cos_ref[...]] * reps, axis=1)
    sin = jnp.concatenate([sin_ref[...]] * reps, axis=1)
    lane = lax.broadcasted_iota(jnp.int32, acc.shape, 1)
    first = (lane % HEAD_DIM) < (HEAD_DIM // 2)
    partner = jnp.where(first, pltpu.roll(acc, acc.shape[1] - HEAD_DIM // 2, 1),
                        pltpu.roll(acc, HEAD_DIM // 2, 1))
    return acc * cos + partner * sin


def _inproj_kernel(x_ref, nw_ref, w_ref, cos_ref, sin_ref, o_ref, hh_ref, hl_ref):
    j = pl.program_id(1)

    @pl.when(j == 0)
    def _():
        h = _rms(x_ref[...], nw_ref[...])
        hi = h.astype(BF16)
        hh_ref[...] = hi
        hl_ref[...] = (h - hi.astype(F32)).astype(BF16)

    wh, wl = _hi_lo(w_ref[...])
    hh = hh_ref[...]
    acc = _dg(hh, wh, _NN) + _dg(hh, wl, _NN) + _dg(hl_ref[...], wh, _NN)
    is_rope = jnp.logical_or(j == C_DQ // COL_TILE, j == C_DK // COL_TILE)

    @pl.when(is_rope)
    def _():
        o_ref[...] = _rotary(acc, cos_ref, sin_ref)

    @pl.when(jnp.logical_not(is_rope))
    def _():
        o_ref[...] = acc


def _inproj(x, norm_w, w, cos_t, sin_t):
    t = x.shape[0]
    return pl.pallas_call(
        _inproj_kernel,
        grid=(1, PROJ_W // COL_TILE),
        in_specs=[
            pl.BlockSpec((t, D_MODEL), lambda i, j: (0, 0)),
            pl.BlockSpec((1, D_MODEL), lambda i, j: (0, 0)),
            pl.BlockSpec((D_MODEL, COL_TILE), lambda i, j: (0, j)),
            pl.BlockSpec((t, LANES), lambda i, j: (0, 0)),
            pl.BlockSpec((t, LANES), lambda i, j: (0, 0)),
        ],
        out_specs=pl.BlockSpec((t, COL_TILE), lambda i, j: (0, j)),
        out_shape=jax.ShapeDtypeStruct((t, PROJ_W), F32),
        scratch_shapes=[pltpu.VMEM((t, D_MODEL), BF16), pltpu.VMEM((t, D_MODEL), BF16)],
        compiler_params=_cparams(("parallel", "arbitrary")),
    )(x, norm_w, w, cos_t, sin_t)


_T_SBK, _T_SBV, _T_DK, _T_DV = (c // COL_TILE for c in (C_SBK, C_SBV, C_DK, C_DV))


def _inproj_prompt_kernel(x_ref, nw_ref, w_ref, wt_ref, cos_ref, sin_ref, *refs):
    proj_ref, kt_ref, vt_ref, dk_ref, dv_ref, hh_ref = refs[-6:]
    j = pl.program_id(1)

    @pl.when(j == 0)
    def _():
        hh_ref[...] = _rms(x_ref[...], nw_ref[...]).astype(BF16)

    transposed = jnp.logical_or(j == _T_SBK, j == _T_SBV)

    @pl.when(jnp.logical_not(transposed))
    def _():
        acc = _dg(hh_ref[...], w_ref[...], _NN)

        @pl.when(j == C_DQ // COL_TILE)
        def _():
            proj_ref[...] = _rotary(acc, cos_ref, sin_ref)

        @pl.when(jnp.logical_and(j < PROMPT_W // COL_TILE, j != C_DQ // COL_TILE))
        def _():
            proj_ref[...] = acc

        @pl.when(j == _T_DK)
        def _():
            dk_ref[0] = _rotary(acc, cos_ref, sin_ref)

        @pl.when(j == _T_DV)
        def _():
            dv_ref[0] = acc

    @pl.when(j == _T_SBK)
    def _():
        kt_ref[0, 0] = _dg(wt_ref[...], hh_ref[...], _NT)

    @pl.when(j == _T_SBV)
    def _():
        vt_ref[0, 0] = _dg(wt_ref[...], hh_ref[...], _NT)


def _inproj_prompt(x, norm_w, w, w_kv_t, cos_t, sin_t, rows, layer, *, depth, batch, seq, tm):
    t = x.shape[0]
    nt = seq // tm
    last_proj = PROMPT_W // COL_TILE - 1
    any_spec = pl.BlockSpec(memory_space=pl.ANY)
    t_spec = pl.BlockSpec((1, 1, W_BRANCH, tm), lambda i, j: (layer, i // nt, 0, i % nt))
    r_spec = pl.BlockSpec((1, tm, W_BRANCH), lambda i, j: (layer, i, 0))
    t_shape = jax.ShapeDtypeStruct((depth, batch, W_BRANCH, seq), F32)
    r_shape = jax.ShapeDtypeStruct((depth, t, W_BRANCH), F32)
    n_fixed = 6
    return pl.pallas_call(
        _inproj_prompt_kernel,
        grid=(t // tm, PROJ_W // COL_TILE),
        in_specs=[
            pl.BlockSpec((tm, D_MODEL), lambda i, j: (i, 0)),
            pl.BlockSpec((1, D_MODEL), lambda i, j: (0, 0)),
            pl.BlockSpec((D_MODEL, COL_TILE),
                         lambda i, j: (0, jnp.where(j <= last_proj, j, jnp.where(j < _T_DK, last_proj, j)))),
            pl.BlockSpec((COL_TILE, D_MODEL), lambda i, j: (jnp.clip(j - _T_SBK, 0, 1), 0)),
            pl.BlockSpec((tm, LANES), lambda i, j: (i % nt, 0)),
            pl.BlockSpec((tm, LANES), lambda i, j: (i % nt, 0)),
        ] + ([] if rows is None else [any_spec] * 4),
        out_specs=[pl.BlockSpec((tm, COL_TILE), lambda i, j: (i, jnp.minimum(j, last_proj))),
                   t_spec, t_spec, r_spec, r_spec],
        out_shape=[jax.ShapeDtypeStruct((t, PROMPT_W), F32), t_shape, t_shape, r_shape, r_shape],
        input_output_aliases={} if rows is None else {n_fixed + k: 1 + k for k in range(4)},
        scratch_shapes=[pltpu.VMEM((tm, D_MODEL), BF16)],
        compiler_params=_cparams(("parallel", "arbitrary")),
    )(x, norm_w, w, w_kv_t, cos_t, sin_t, *(() if rows is None else rows))


def _sb_prompt_kernel(q_ref, k_ref, v_ref, o_ref, kb_ref, vb_ref, acc_ref, *, tq):
    qi = pl.program_id(2)
    lane_q = lax.broadcasted_iota(jnp.int32, (tq, LANES), 1)

    @pl.when(qi == 0)
    def _():
        sub = lax.broadcasted_iota(jnp.int32, (LANES, 1), 0)
        v = v_ref[0, 0]
        kb_ref[...] = k_ref[0, 0].astype(BF16)
        vb_ref[0] = jnp.where(sub < HEAD_DIM, v, 0.0).astype(BF16)
        vb_ref[1] = jnp.where(sub >= HEAD_DIM, v, 0.0).astype(BF16)

    q = q_ref[...] * (1.0 / math.sqrt(HEAD_DIM))
    q_heads = (jnp.where(lane_q < HEAD_DIM, q, 0.0).astype(BF16),
               jnp.where(lane_q >= HEAD_DIM, q, 0.0).astype(BF16))
    row = lax.broadcasted_iota(jnp.int32, (tq, tq), 0)
    col = lax.broadcasted_iota(jnp.int32, (tq, tq), 1)
    later = jnp.where(row > col, 1.0, 0.0).astype(BF16)
    strictly_causal = col < row
    acc_ref[...] = jnp.zeros_like(acc_ref)

    def block(kb, carry, diagonal):
        start = pl.multiple_of(kb * tq, tq)
        k = kb_ref[:, pl.ds(start, tq)]
        new_carry = []
        contrib = None
        for hd in range(2):
            z = _dg(q_heads[hd], k, _NN)
            sp = _softplus(z)
            if diagonal:
                sp = jnp.where(strictly_causal, sp, 0.0)
            after = _mm_exact_rhs(sp, later)
            total = after[:, 0:1] + sp[:, 0:1]
            w = jnp.exp(z - sp - after - carry[hd])
            if diagonal:
                w = jnp.where(strictly_causal, w, 0.0)
            pv = _dg(w.astype(BF16), vb_ref[hd, :, pl.ds(start, tq)], _NT)
            contrib = pv if contrib is None else contrib + pv
            new_carry.append(carry[hd] + total)
        acc_ref[...] += contrib
        return tuple(new_carry)

    zero = jnp.zeros((tq, 1), F32)
    c0, c1 = block(qi, (zero, zero), True)

    def more(state):
        it, a, b = state
        return jnp.logical_and(it < qi, jnp.min(jnp.minimum(a, b)) < SB_UNDERFLOW)

    def step(state):
        it, a, b = state
        a, b = block(qi - 1 - it, (a, b), False)
        return it + 1, a, b

    lax.while_loop(more, step, (jnp.int32(0), c0, c1))
    o_ref[...] = acc_ref[...].astype(o_ref.dtype)


def _sb_prompt(proj, k_t, v_t, layer, *, batch, seq, tq):
    nq = seq // tq
    t = batch * seq
    cq = C_SBQ // LANES
    kv_spec = pl.BlockSpec((1, 1, LANES, seq), lambda b, hp, qi: (layer, b, hp, 0))
    return pl.pallas_call(
        functools.partial(_sb_prompt_kernel, tq=tq),
        grid=(batch, W_BRANCH // LANES, nq),
        in_specs=[pl.BlockSpec((tq, LANES), lambda b, hp, qi: (b * nq + qi, cq + hp)), kv_spec, kv_spec],
        out_specs=pl.BlockSpec((tq, LANES), lambda b, hp, qi: (b * nq + qi, hp)),
        out_shape=jax.ShapeDtypeStruct((t, W_BRANCH), BF16),
        scratch_shapes=[pltpu.VMEM((LANES, seq), BF16), pltpu.VMEM((2, LANES, seq), BF16),
                        pltpu.VMEM((tq, LANES), F32)],
        compiler_params=_cparams(("parallel", "parallel", "arbitrary")),
    )(proj, k_t, v_t)


def _lambda_value(lam_ref, lam_init):
    lv = lam_ref[...]
    s1 = jnp.sum(lv[0:1, :] * lv[1:2, :], axis=-1, keepdims=True)
    s2 = jnp.sum(lv[2:3, :] * lv[3:4, :], axis=-1, keepdims=True)
    return jnp.exp(s1) - jnp.exp(s2) + lam_init


def _diff_prompt_kernel(q_ref, k_ref, v_ref, lam_ref, nw_ref, o_ref, kb_ref, vb_ref, acc_ref, m_ref, l_ref,
                        *, tq, big, lam_init):
    qi = pl.program_id(2)

    @pl.when(qi == 0)
    def _():
        kb_ref[...] = k_ref[0].astype(BF16)
        vb_ref[...] = v_ref[0].astype(BF16)

    q = q_ref[...] * (LOG2E / math.sqrt(HEAD_DIM))
    lane_q = lax.broadcasted_iota(jnp.int32, (tq, LANES), 1)
    q_comp = (jnp.where(lane_q < HEAD_DIM, q, 0.0).astype(BF16),
              jnp.where(lane_q >= HEAD_DIM, q, 0.0).astype(BF16))
    row = lax.broadcasted_iota(jnp.int32, (tq, tq), 0)
    col = lax.broadcasted_iota(jnp.int32, (tq, tq), 1)
    causal = col <= row

    def block(start, size, diagonal, first):
        start = pl.multiple_of(start, tq)
        k = kb_ref[pl.ds(start, size), :]
        v = vb_ref[pl.ds(start, size), :]
        for c in range(2):
            s = _dg(q_comp[c], k, _NT)
            if diagonal:
                s = jnp.where(causal, s, -jnp.inf)
            blk_max = jnp.max(s, axis=-1, keepdims=True)
            if first:
                m_new = jnp.broadcast_to(blk_max, (tq, LANES))
            else:
                m_old = m_ref[c]
                m_new = jnp.maximum(m_old, blk_max)
                alpha = jnp.exp2(m_old - m_new)
            p = jnp.exp2(s - jnp.concatenate([m_new] * (size // LANES), axis=1))
            p_sum = jnp.sum(p, axis=-1, keepdims=True)
            pv = _dg(p.astype(BF16), v, _NN)
            if first:
                l_ref[c] = jnp.broadcast_to(p_sum, (tq, LANES))
                acc_ref[c] = pv
            else:
                l_ref[c] = alpha * l_ref[c] + p_sum
                acc_ref[c] = alpha * acc_ref[c] + pv
            m_ref[c] = m_new

    block(qi * tq, tq, True, True)
    n_big = (qi * tq) // big
    n_small = (qi * tq - n_big * big) // tq

    def big_body(it, _):
        block(it * big, big, False, False)
        return 0

    def small_body(it, _):
        block(n_big * big + it * tq, tq, False, False)
        return 0

    lax.fori_loop(0, n_big, big_body, 0)
    lax.fori_loop(0, n_small, small_body, 0)
    lam = _lambda_value(lam_ref, lam_init)
    o = acc_ref[0] / l_ref[0] - lam * (acc_ref[1] / l_ref[1])
    o_ref[...] = (_rms(o, nw_ref[...]) * (1.0 - lam_init)).astype(o_ref.dtype)


def _diff_prompt(proj, k_rows, v_rows, layer, lam_vecs, diff_norm_w, *, batch, seq, tq, lam_init):
    nq = seq // tq
    t = batch * seq
    cq = C_DQ // LANES
    big = min(DIFF_KEY_CHUNK, seq)
    kv_spec = pl.BlockSpec((1, seq, LANES), lambda b, hh, qi: (layer, b, hh))
    return pl.pallas_call(
        functools.partial(_diff_prompt_kernel, tq=tq, big=big, lam_init=lam_init),
        grid=(batch, W_BRANCH // LANES, nq),
        in_specs=[
            pl.BlockSpec((tq, LANES), lambda b, hh, qi: (b * nq + qi, cq + hh)),
            kv_spec, kv_spec,
            pl.BlockSpec((4, HEAD_DIM), lambda b, hh, qi: (0, 0)),
            pl.BlockSpec((1, LANES), lambda b, hh, qi: (0, 0)),
        ],
        out_specs=pl.BlockSpec((tq, LANES), lambda b, hh, qi: (b * nq + qi, hh)),
        out_shape=jax.ShapeDtypeStruct((t, W_BRANCH), BF16),
        scratch_shapes=[pltpu.VMEM((seq, LANES), BF16), pltpu.VMEM((seq, LANES), BF16),
                        pltpu.VMEM((2, tq, LANES), F32), pltpu.VMEM((2, tq, LANES), F32),
                        pltpu.VMEM((2, tq, LANES), F32)],
        compiler_params=_cparams(("parallel", "parallel", "arbitrary")),
    )(proj, k_rows, v_rows, lam_vecs, diff_norm_w)


def _ssd_prompt_kernel(z_ref, xa_ref, xb_ref, cw_ref, cb_ref, dtb_ref, alog_ref, dsk_ref, nw_ref,
                       y_ref, hout_ref, cout_ref, xpad_ref, h_ref, *, n_heads, n_groups):
    c = pl.program_id(1)
    nc = pl.num_programs(1)
    L = SSM_CHUNK
    P = HEAD_DIM
    N = SSM_STATE
    xw = n_heads * P
    gn = n_groups * N

    @pl.when(c == 0)
    def _():
        xpad_ref[0:8, :] = jnp.zeros((8, xpad_ref.shape[1]), F32)
        h_ref[...] = jnp.zeros_like(h_ref)

    xb = xb_ref[...]
    xbc = jnp.concatenate([xa_ref[...], xb[:, :2 * gn]], axis=1)
    xpad_ref[8:8 + L, :] = xbc
    conv = cb_ref[...]
    for tap in range(CONV_TAPS):
        conv = conv + xpad_ref[pl.ds(8 - (CONV_TAPS - 1) + tap, L), :] * cw_ref[tap:tap + 1, :]
    conv = conv * jax.nn.sigmoid(conv)
    xpad_ref[0:8, :] = xbc[L - 8:, :]

    @pl.when(c == nc - 1)
    def _():
        cout_ref[0] = xbc[L - (CONV_TAPS - 1):, :]

    xs = conv[:, :xw]
    bmat = conv[:, xw:xw + gn]
    cmat = conv[:, xw + gn:]

    lane = lax.broadcasted_iota(jnp.int32, (1, LANES), 1)
    head_lane = lane < n_heads
    dt = jnp.where(head_lane, _softplus(xb[:, 2 * gn:2 * gn + LANES] + dtb_ref[...]), 0.0)
    a = jnp.where(head_lane, -jnp.exp(alog_ref[...]), 0.0)
    dta = dt * a
    ri = lax.broadcasted_iota(jnp.int32, (L, L), 0)
    ci = lax.broadcasted_iota(jnp.int32, (L, L), 1)
    lower = jnp.where(ri >= ci, 1.0, 0.0).astype(BF16)
    dta_h, dta_l = _hi_lo(dta)
    dta_l2 = (dta - dta_h.astype(F32) - dta_l.astype(F32)).astype(BF16)
    cum = _dg(lower, dta_h, _NN) + _dg(lower, dta_l, _NN) + _dg(lower, dta_l2, _NN)
    cum_t = cum.T
    dt_t = dt.T
    causal = ri >= ci

    y_parts = []
    for g in range(n_groups):
        b_g = bmat[:, g * N:(g + 1) * N]
        c_g = cmat[:, g * N:(g + 1) * N]
        cb = _mm(c_g, b_g, False, _NT)
        for hh in range(n_heads // n_groups):
            hd = g * (n_heads // n_groups) + hh
            x_h = xs[:, hd * P:(hd + 1) * P]
            cum_col = cum[:, hd:hd + 1]
            cum_row = cum_t[hd:hd + 1, :]
            dt_col = dt[:, hd:hd + 1]
            dt_row = dt_t[hd:hd + 1, :]
            decay = jnp.exp(jnp.where(causal, cum_col - cum_row, -jnp.inf))
            y_intra = _mm(cb * decay * dt_row, x_h, False)
            h_prev = h_ref[hd]
            y_inter = _mm(c_g, h_prev, False, _NT) * jnp.exp(cum_col)
            cum_end = cum[L - 1:L, hd:hd + 1]
            w_end = jnp.exp(cum_end - cum_col) * dt_col
            s_chunk = _mm(x_h * w_end, b_g, False, _TN)
            h_ref[hd] = jnp.exp(cum_end) * h_prev + s_chunk
            y_parts.append(y_intra + y_inter + dsk_ref[0:1, hd:hd + 1] * x_h)
    y = jnp.concatenate(y_parts, axis=1)
    zg = z_ref[...]
    y = y * (zg * jax.nn.sigmoid(zg))
    gw = xw // n_groups
    nw = nw_ref[...]
    y = jnp.concatenate([_rms(y[:, g * gw:(g + 1) * gw], nw[:, g * gw:(g + 1) * gw])
                         for g in range(n_groups)], axis=1)
    y_ref[...] = y.astype(y_ref.dtype)

    @pl.when(c == nc - 1)
    def _():
        hout_ref[0] = h_ref[...]


def _ssd_prompt(proj, conv_w, conv_b, dt_bias, a_log, d_skip, norm_w, *, batch, seq, n_heads, n_groups):
    L = SSM_CHUNK
    nc = seq // L
    t = batch * seq
    cdim = conv_w.shape[1]
    cz, cx = C_Z // COL_TILE, C_XBC // COL_TILE
    full = lambda shape: pl.BlockSpec(shape, lambda b, c: (0,) * len(shape))
    return pl.pallas_call(
        functools.partial(_ssd_prompt_kernel, n_heads=n_heads, n_groups=n_groups),
        grid=(batch, nc),
        in_specs=[
            pl.BlockSpec((L, COL_TILE), lambda b, c: (b * nc + c, cz)),
            pl.BlockSpec((L, COL_TILE), lambda b, c: (b * nc + c, cx)),
            pl.BlockSpec((L, COL_TILE), lambda b, c: (b * nc + c, cx + 1)),
            full((CONV_TAPS, cdim)), full((1, cdim)), full((1, LANES)), full((1, LANES)),
            full((1, LANES)), full((1, W_BRANCH)),
        ],
        out_specs=[
            pl.BlockSpec((L, W_BRANCH), lambda b, c: (b * nc + c, 0)),
            pl.BlockSpec((1, n_heads, HEAD_DIM, SSM_STATE), lambda b, c: (b, 0, 0, 0)),
            pl.BlockSpec((1, CONV_TAPS - 1, cdim), lambda b, c: (b, 0, 0)),
        ],
        out_shape=[
            jax.ShapeDtypeStruct((t, W_BRANCH), BF16),
            jax.ShapeDtypeStruct((batch, n_heads, HEAD_DIM, SSM_STATE), F32),
            jax.ShapeDtypeStruct((batch, CONV_TAPS - 1, cdim), F32),
        ],
        scratch_shapes=[pltpu.VMEM((8 + L, cdim), F32), pltpu.VMEM((n_heads, HEAD_DIM, SSM_STATE), F32)],
        compiler_params=_cparams(("parallel", "arbitrary")),
    )(proj, proj, proj, conv_w, conv_b, dt_bias, a_log, d_skip, norm_w)


def _merge_kernel(x_ref, g0_ref, g1_ref, g2_ref, a_ref, b_ref, c_ref, wa_ref, wb_ref, wc_ref, wo_ref,
                  o_ref, *, precise):
    m = (jax.nn.sigmoid(g0_ref[...]) * _mm(a_ref[...], wa_ref[...], precise)
         + jax.nn.sigmoid(g1_ref[...]) * _mm(b_ref[...], wb_ref[...], precise)
         + jax.nn.sigmoid(g2_ref[...]) * _mm(c_ref[...], wc_ref[...], precise))
    o_ref[...] = x_ref[...] + _mm(m, wo_ref[...], precise)


def _merge(x, proj, o_sb, y_ssm, o_diff, w_sb, w_ssm, w_diff, w_o, *, precise, tm):
    t = x.shape[0]
    full = lambda shape: pl.BlockSpec(shape, lambda i: (0, 0))
    gate = lambda n: pl.BlockSpec((tm, D_MODEL), lambda i: (i, C_GATE // D_MODEL + n))
    branch = pl.BlockSpec((tm, W_BRANCH), lambda i: (i, 0))
    return pl.pallas_call(
        functools.partial(_merge_kernel, precise=precise),
        grid=(t // tm,),
        in_specs=[pl.BlockSpec((tm, D_MODEL), lambda i: (i, 0)), gate(0), gate(1), gate(2),
                  branch, branch, branch,
                  full((W_BRANCH, D_MODEL)), full((W_BRANCH, D_MODEL)), full((W_BRANCH, D_MODEL)),
                  full((D_MODEL, D_MODEL))],
        out_specs=pl.BlockSpec((tm, D_MODEL), lambda i: (i, 0)),
        out_shape=jax.ShapeDtypeStruct((t, D_MODEL), F32),
        compiler_params=_cparams(("parallel",)),
    )(x, proj, proj, proj, o_sb, y_ssm, o_diff, w_sb, w_ssm, w_diff, w_o)


def _route(logits):
    lane = lax.broadcasted_iota(jnp.int32, logits.shape, 1)
    neg = -jnp.inf
    big = jnp.int32(1 << 20)
    g = jnp.where(lane < MOE_GROUPS, logits, neg)
    g_max = jnp.max(g, axis=-1, keepdims=True)
    g_sel = jnp.min(jnp.where(g == g_max, lane, big), axis=-1, keepdims=True)
    g_w = 1.0 / jnp.sum(jnp.exp(g - g_max), axis=-1, keepdims=True)
    lo = MOE_GROUPS + g_sel * MOE_PER_GROUP
    e = jnp.where((lane >= lo) & (lane < lo + MOE_PER_GROUP), logits, neg)
    v1 = jnp.max(e, axis=-1, keepdims=True)
    i1 = jnp.min(jnp.where(e == v1, lane, big), axis=-1, keepdims=True)
    e2 = jnp.where(lane == i1, neg, e)
    v2 = jnp.max(e2, axis=-1, keepdims=True)
    i2 = jnp.min(jnp.where(e2 == v2, lane, big), axis=-1, keepdims=True)
    r = jnp.exp(v2 - v1)
    w1 = g_w / (1.0 + r)
    w2 = g_w * r / (1.0 + r)
    return jnp.where(lane == i1, w1, 0.0) + jnp.where(lane == i2, w2, 0.0)


def _moe_kernel(x_ref, nw_ref, wr_ref, wg_ref, wu_ref, wd_ref, fw_ref, o_ref,
                hh_ref, hl_ref, comb_ref, acc_ref, *, precise, final):
    e = pl.program_id(1)
    ne = pl.num_programs(1)

    @pl.when(e == 0)
    def _():
        h = _rms(x_ref[...], nw_ref[...])
        hi = h.astype(BF16)
        lo = (h - hi.astype(F32)).astype(BF16)
        hh_ref[...] = hi
        hl_ref[...] = lo
        wrh, wrl = _hi_lo(wr_ref[...])
        logits = _dg(hi, wrh, _NN) + _dg(hi, wrl, _NN) + _dg(lo, wrh, _NN)
        comb_ref[...] = _route(logits)
        acc_ref[...] = jnp.zeros_like(acc_ref)

    hh = hh_ref[...]
    if precise:
        hl = hl_ref[...]
        gh, gl = _hi_lo(wg_ref[0, 0])
        uh, ul = _hi_lo(wu_ref[0, 0])
        gate = _dg(hh, gh, _NN) + _dg(hh, gl, _NN) + _dg(hl, gh, _NN)
        up = _dg(hh, uh, _NN) + _dg(hh, ul, _NN) + _dg(hl, uh, _NN)
    else:
        gate = _dg(hh, wg_ref[0, 0], _NN)
        up = _dg(hh, wu_ref[0, 0], _NN)
    lane = lax.broadcasted_iota(jnp.int32, comb_ref.shape, 1)
    cw = jnp.sum(jnp.where(lane == e + MOE_GROUPS, comb_ref[...], 0.0), axis=-1, keepdims=True)
    he = gate * jax.nn.sigmoid(gate) * up * cw
    acc_ref[...] += _mm(he, wd_ref[0, 0], precise)

    @pl.when(e == ne - 1)
    def _():
        y = x_ref[...] + acc_ref[...]
        if final:
            y = _rms(y, fw_ref[...])
        o_ref[...] = y


def _moe(x, norm_w, w_route, w_gate, w_up, w_down, final_w, layer, *, precise, final, tm):
    t = x.shape[0]
    _, n_exp, _, ff = w_gate.shape
    return pl.pallas_call(
        functools.partial(_moe_kernel, precise=precise, final=final),
        grid=(t // tm, n_exp),
        in_specs=[
            pl.BlockSpec((tm, D_MODEL), lambda i, e: (i, 0)),
            pl.BlockSpec((1, D_MODEL), lambda i, e: (0, 0)),
            pl.BlockSpec((D_MODEL, LANES), lambda i, e: (0, 0)),
            pl.BlockSpec((1, 1, D_MODEL, ff), lambda i, e: (layer, e, 0, 0)),
            pl.BlockSpec((1, 1, D_MODEL, ff), lambda i, e: (layer, e, 0, 0)),
            pl.BlockSpec((1, 1, ff, D_MODEL), lambda i, e: (layer, e, 0, 0)),
            pl.BlockSpec((1, D_MODEL), lambda i, e: (0, 0)),
        ],
        out_specs=pl.BlockSpec((tm, D_MODEL), lambda i, e: (i, 0)),
        out_shape=jax.ShapeDtypeStruct((t, D_MODEL), F32),
        scratch_shapes=[pltpu.VMEM((tm, D_MODEL), BF16), pltpu.VMEM((tm, D_MODEL), BF16),
                        pltpu.VMEM((tm, LANES), F32), pltpu.VMEM((tm, D_MODEL), F32)],
        compiler_params=_cparams(("parallel", "arbitrary")),
    )(x, norm_w, w_route, w_gate, w_up, w_down, final_w)


def _row_to_col(row, eye):
    return jnp.sum(jnp.where(eye, jnp.broadcast_to(row, eye.shape), 0.0), axis=1, keepdims=True)


def _col_to_row(col, eye):
    return jnp.sum(jnp.where(eye, jnp.broadcast_to(col, eye.shape), 0.0), axis=0, keepdims=True)


def _sb_sample_kernel(*refs, pages_per_step, n_heads, resume):
    if resume:
        pt_ref, more_ref, q_ref, prev_o_ref, prev_c_ref = refs[:5]
        refs = refs[5:]
    else:
        pt_ref, q_ref = refs[:2]
        refs = refs[2:]
    k_refs = refs[:pages_per_step]
    v_refs = refs[pages_per_step:2 * pages_per_step]
    if resume:
        o_ref, qb_ref, acc_ref, c_ref = refs[2 * pages_per_step:]
    else:
        o_ref, c_out_ref, more_out_ref, qb_ref, acc_ref, c_ref = refs[2 * pages_per_step:]
    b = pl.program_id(0)
    j = pl.program_id(1)
    nj = pl.num_programs(1)
    P = HEAD_DIM
    page = qb_ref.shape[-1]
    eye = lax.broadcasted_iota(jnp.int32, (P, P), 0) == lax.broadcasted_iota(jnp.int32, (P, P), 1)

    @pl.when(j == 0)
    def _():
        q = q_ref[0] * (1.0 / math.sqrt(HEAD_DIM))
        for hd in range(n_heads):
            qb_ref[hd] = jnp.broadcast_to(_row_to_col(q[:, hd * P:(hd + 1) * P], eye), (P, page))
        acc_ref[...] = jnp.zeros_like(acc_ref)
        c_ref[...] = prev_c_ref[0] if resume else jnp.zeros_like(c_ref)

    def walk():
        ri = lax.broadcasted_iota(jnp.int32, (page, page), 0)
        ci = lax.broadcasted_iota(jnp.int32, (page, page), 1)
        later = jnp.where(ri > ci, 1.0, 0.0).astype(BF16)
        carry = c_ref[:, 0:1]
        for p in range(pages_per_step):
            z = jnp.concatenate([jnp.sum(k_refs[p][0, 0, hd] * qb_ref[hd], axis=0, keepdims=True)
                                 for hd in range(n_heads)], axis=0)
            sp = _softplus(z)
            sph, spl = _hi_lo(sp)
            spl2 = (sp - sph.astype(F32) - spl.astype(F32)).astype(BF16)
            after = _dg(sph, later, _NN) + _dg(spl, later, _NN) + _dg(spl2, later, _NN)
            w = jnp.exp(z - sp - after - carry)
            for hd in range(n_heads):
                acc_ref[hd] += v_refs[p][0, 0, hd] * w[hd:hd + 1, :]
            carry = carry + after[:, 0:1] + sp[:, 0:1]
        c_ref[...] = jnp.broadcast_to(carry, c_ref.shape)

    if resume:
        pl.when(more_ref[b] != 0)(walk)
    else:
        walk()

    @pl.when(j == nj - 1)
    def _():
        o = jnp.concatenate(
            [_col_to_row(jnp.sum(acc_ref[hd], axis=1, keepdims=True), eye) for hd in range(n_heads)], axis=1)
        if resume:
            o_ref[0] = o + prev_o_ref[0]
        else:
            o_ref[0] = o
            carry = c_ref[...]
            c_out_ref[0] = carry
            more_out_ref[0] = jnp.broadcast_to(jnp.where(jnp.min(carry) < SB_UNDERFLOW, 1, 0).astype(jnp.int32),
                                               (1, LANES))


def _sb_sample_walk(page_table, q, cache_k_t, cache_v_t, layer, newest, n_steps, resume_args, *, pages_per_step):
    nb, n_pages = page_table.shape
    _, _, n_heads, hdim, page = cache_k_t.shape
    width = n_heads * hdim
    pt_flat = page_table.reshape(-1)
    resume = resume_args is not None

    def page_spec(p):
        def index(b, j, pt, *more):
            phys = pt[b * n_pages + newest - j * pages_per_step - p]
            if resume:
                phys = jnp.where(more[0][b] != 0, phys, 0)
            return (layer, phys, 0, 0, 0)
        return pl.BlockSpec((1, 1, n_heads, hdim, page), index)

    specs = [page_spec(p) for p in range(pages_per_step)]
    row = pl.BlockSpec((1, 1, width), lambda b, j, *_: (b, 0, 0))
    carry_spec = pl.BlockSpec((1, n_heads, LANES), lambda b, j, *_: (b, 0, 0))
    flag_spec = pl.BlockSpec((1, 1, LANES), lambda b, j, *_: (b, 0, 0))
    scratch = [pltpu.VMEM((n_heads, hdim, page), F32), pltpu.VMEM((n_heads, hdim, page), F32),
               pltpu.VMEM((n_heads, LANES), F32)]
    kernel_fn = functools.partial(_sb_sample_kernel, pages_per_step=pages_per_step, n_heads=n_heads, resume=resume)
    pages = [cache_k_t] * pages_per_step + [cache_v_t] * pages_per_step
    if resume:
        more, prev_o, prev_c = resume_args
        grid_spec = pltpu.PrefetchScalarGridSpec(
            num_scalar_prefetch=2, grid=(nb, n_steps),
            in_specs=[row, row, carry_spec] + specs + specs, out_specs=row, scratch_shapes=scratch)
        return pl.pallas_call(
            kernel_fn, grid_spec=grid_spec, out_shape=jax.ShapeDtypeStruct((nb, 1, width), F32),
            compiler_params=_cparams(("parallel", "arbitrary")),
        )(pt_flat, more, q, prev_o, prev_c, *pages)
    grid_spec = pltpu.PrefetchScalarGridSpec(
        num_scalar_prefetch=1, grid=(nb, n_steps),
        in_specs=[row] + specs + specs, out_specs=[row, carry_spec, flag_spec], scratch_shapes=scratch)
    return pl.pallas_call(
        kernel_fn, grid_spec=grid_spec,
        out_shape=[jax.ShapeDtypeStruct((nb, 1, width), F32), jax.ShapeDtypeStruct((nb, n_heads, LANES), F32),
                   jax.ShapeDtypeStruct((nb, 1, LANES), jnp.int32)],
        compiler_params=_cparams(("parallel", "arbitrary")),
    )(pt_flat, q, *pages)


def _sb_sample(page_table, q, cache_k_t, cache_v_t, layer, *, pages_per_step):
    n_pages = page_table.shape[1]
    o, carry, more = _sb_sample_walk(page_table, q, cache_k_t, cache_v_t, layer, n_pages - 1, 1, None,
                                     pages_per_step=pages_per_step)
    older = n_pages - pages_per_step
    if older == 0:
        return o
    wide = max(p for p in range(1, SB_OLDER_PAGES_PER_STEP + 1) if older % p == 0)
    return _sb_sample_walk(page_table, q, cache_k_t, cache_v_t, layer, older - 1, older // wide,
                           (more[:, 0, 0], o, carry), pages_per_step=wide)


def _diff_sample_kernel(pt_ref, q_ref, kn_ref, vn_ref, lam_ref, nw_ref, *refs, pages_per_step, n_heads, lam_init):
    k_refs = refs[:pages_per_step]
    v_refs = refs[pages_per_step:2 * pages_per_step]
    o_ref, q8_ref, acc_ref, m_ref, l_ref = refs[2 * pages_per_step:]
    j = pl.program_id(1)
    nj = pl.num_programs(1)
    rows = k_refs[0].shape[2]
    comp_of_lane = lax.broadcasted_iota(jnp.int32, (LANES, LANES), 0) // HEAD_DIM
    sel = [jnp.where(comp_of_lane == c, 1.0, 0.0).astype(BF16) for c in range(2)]
    row8 = lax.broadcasted_iota(jnp.int32, (8, LANES), 0)

    def per_sublane(vec):
        return jnp.concatenate([vec[:, (i % n_heads) * LANES:(i % n_heads + 1) * LANES] for i in range(8)], axis=0)

    @pl.when(j == 0)
    def _():
        q8 = per_sublane(q_ref[0]) * (LOG2E / math.sqrt(HEAD_DIM))
        q8_ref[...] = q8
        prod = per_sublane(kn_ref[0]) * q8
        v_new = jnp.where(row8 < n_heads, per_sublane(vn_ref[0]), 0.0)
        for c in range(2):
            m_ref[c] = _mm_exact_rhs(prod, sel[c])
            l_ref[c] = jnp.where(row8 < n_heads, 1.0, 0.0)
            acc_ref[c] = v_new

    q8 = q8_ref[...]
    sel_both = jnp.concatenate([jnp.concatenate(sel, axis=1)] * 2, axis=0)
    for p in range(pages_per_step):
        k3 = k_refs[p][0, 0].reshape(rows // 8, 8, LANES)
        v3 = v_refs[p][0, 0].reshape(rows // 8, 8, LANES)
        prod_hi, prod_lo = _hi_lo((k3 * q8[None]).reshape(rows, LANES))
        s_both = _dg(jnp.concatenate([prod_hi, prod_lo], axis=1), sel_both, _NN)
        for c in range(2):
            s = s_both[:, c * LANES:(c + 1) * LANES].reshape(rows // 8, 8, LANES)
            blk_max = jnp.max(s, axis=0)
            blk_max = jnp.maximum(blk_max, pltpu.roll(blk_max, n_heads, 0))
            m_old = m_ref[c]
            m_new = jnp.maximum(m_old, blk_max)
            alpha = jnp.exp2(m_old - m_new)
            pr = jnp.exp2(s - m_new[None])
            l_ref[c] = alpha * l_ref[c] + jnp.sum(pr, axis=0)
            acc_ref[c] = alpha * acc_ref[c] + jnp.sum(pr * v3, axis=0)
            m_ref[c] = m_new

    @pl.when(j == nj - 1)
    def _():
        lam = _lambda_value(lam_ref, lam_init)
        outs = []
        for c in range(2):
            l = l_ref[c]
            a = acc_ref[c]
            outs.append((a + pltpu.roll(a, n_heads, 0)) / (l + pltpu.roll(l, n_heads, 0)))
        o = _rms(outs[0] - lam * outs[1], nw_ref[...]) * (1.0 - lam_init)
        o_ref[0] = jnp.concatenate([o[hd:hd + 1, :] for hd in range(n_heads)], axis=1)


def _diff_sample(page_table, q, k_new, v_new, lam_vecs, diff_norm_w, cache_k_r, cache_v_r, layer, *,
                 pages_per_step, n_heads, lam_init):
    nb, n_pages = page_table.shape
    rows = cache_k_r.shape[2]
    width = n_heads * LANES
    nj = n_pages // pages_per_step
    pt_flat = page_table.reshape(-1)

    def page_spec(p):
        def index(b, j, pt):
            return (layer, pt[b * n_pages + j * pages_per_step + p], 0, 0)
        return pl.BlockSpec((1, 1, rows, LANES), index)

    specs = [page_spec(p) for p in range(pages_per_step)]
    row = pl.BlockSpec((1, 1, width), lambda b, j, pt: (b, 0, 0))
    grid_spec = pltpu.PrefetchScalarGridSpec(
        num_scalar_prefetch=1,
        grid=(nb, nj),
        in_specs=[row, row, row,
                  pl.BlockSpec((4, HEAD_DIM), lambda b, j, pt: (0, 0)),
                  pl.BlockSpec((1, LANES), lambda b, j, pt: (0, 0))] + specs + specs,
        out_specs=row,
        scratch_shapes=[pltpu.VMEM((8, LANES), F32), pltpu.VMEM((2, 8, LANES), F32),
                        pltpu.VMEM((2, 8, LANES), F32), pltpu.VMEM((2, 8, LANES), F32)],
    )
    return pl.pallas_call(
        functools.partial(_diff_sample_kernel, pages_per_step=pages_per_step, n_heads=n_heads, lam_init=lam_init),
        grid_spec=grid_spec,
        out_shape=jax.ShapeDtypeStruct((nb, 1, width), F32),
        compiler_params=_cparams(("parallel", "arbitrary")),
    )(pt_flat, q, k_new, v_new, lam_vecs, diff_norm_w,
      *([cache_k_r] * pages_per_step), *([cache_v_r] * pages_per_step))


def _ssd_sample_kernel(z_ref, xa_ref, xb_ref, cprev_ref, hprev_ref, cw_ref, cb_ref, dtb_ref, alog_ref, dsk_ref,
                       nw_ref, y_ref, hout_ref, cout_ref, *, n_heads, n_groups):
    P = HEAD_DIM
    N = SSM_STATE
    xw = n_heads * P
    gn = n_groups * N
    xb = xb_ref[0]
    xbc = jnp.concatenate([xa_ref[0], xb[:, :2 * gn]], axis=1)
    prev = cprev_ref[0]
    conv = cb_ref[...] + xbc * cw_ref[CONV_TAPS - 1:CONV_TAPS, :]
    for tap in range(CONV_TAPS - 1):
        conv = conv + prev[tap:tap + 1, :] * cw_ref[tap:tap + 1, :]
    conv = conv * jax.nn.sigmoid(conv)
    cout_ref[0, 0:CONV_TAPS - 2, :] = prev[1:, :]
    cout_ref[0, CONV_TAPS - 2:CONV_TAPS - 1, :] = xbc
    xs = conv[:, :xw]
    dt = _softplus(xb[:, 2 * gn:2 * gn + LANES] + dtb_ref[...])
    a = -jnp.exp(alog_ref[...])
    decay = jnp.exp(dt * a)
    eye = lax.broadcasted_iota(jnp.int32, (P, P), 0) == lax.broadcasted_iota(jnp.int32, (P, P), 1)
    y_parts = []
    for hd in range(n_heads):
        g = hd // (n_heads // n_groups)
        b_g = conv[:, xw + g * N:xw + (g + 1) * N]
        c_g = conv[:, xw + gn + g * N:xw + gn + (g + 1) * N]
        x_h = xs[:, hd * P:(hd + 1) * P]
        x_col = jnp.sum(jnp.where(eye, jnp.broadcast_to(x_h, (P, P)), 0.0), axis=1, keepdims=True)
        h_new = decay[:, hd:hd + 1] * hprev_ref[0, hd] + (dt[:, hd:hd + 1] * x_col) * b_g
        hout_ref[0, hd] = h_new
        y_col = jnp.sum(h_new * c_g, axis=1, keepdims=True)
        y_row = jnp.sum(jnp.where(eye, jnp.broadcast_to(y_col, (P, P)), 0.0), axis=0, keepdims=True)
        y_parts.append(y_row + dsk_ref[0:1, hd:hd + 1] * x_h)
    y = jnp.concatenate(y_parts, axis=1)
    zg = z_ref[0]
    y = y * (zg * jax.nn.sigmoid(zg))
    gw = xw // n_groups
    nw = nw_ref[...]
    y_ref[0] = jnp.concatenate([_rms(y[:, g * gw:(g + 1) * gw], nw[:, g * gw:(g + 1) * gw])
                                for g in range(n_groups)], axis=1)


def _ssd_sample(proj3, conv_prev, ssm_prev, conv_w, conv_b, dt_bias, a_log, d_skip, norm_w, *, n_heads, n_groups):
    nb = proj3.shape[0]
    cdim = conv_w.shape[1]
    cz, cx = C_Z // COL_TILE, C_XBC // COL_TILE
    full = lambda shape: pl.BlockSpec(shape, lambda b: (0,) * len(shape))
    col = lambda cidx: pl.BlockSpec((1, 1, COL_TILE), lambda b: (b, 0, cidx))
    return pl.pallas_call(
        functools.partial(_ssd_sample_kernel, n_heads=n_heads, n_groups=n_groups),
        grid=(nb,),
        in_specs=[col(cz), col(cx), col(cx + 1),
                  pl.BlockSpec((1, CONV_TAPS - 1, cdim), lambda b: (b, 0, 0)),
                  pl.BlockSpec((1, n_heads, HEAD_DIM, SSM_STATE), lambda b: (b, 0, 0, 0)),
                  full((CONV_TAPS, cdim)), full((1, cdim)), full((1, LANES)), full((1, LANES)),
                  full((1, LANES)), full((1, W_BRANCH))],
        out_specs=[pl.BlockSpec((1, 1, W_BRANCH), lambda b: (b, 0, 0)),
                   pl.BlockSpec((1, n_heads, HEAD_DIM, SSM_STATE), lambda b: (b, 0, 0, 0)),
                   pl.BlockSpec((1, CONV_TAPS - 1, cdim), lambda b: (b, 0, 0))],
        out_shape=[jax.ShapeDtypeStruct((nb, 1, W_BRANCH), F32),
                   jax.ShapeDtypeStruct((nb, n_heads, HEAD_DIM, SSM_STATE), F32),
                   jax.ShapeDtypeStruct((nb, CONV_TAPS - 1, cdim), F32)],
        compiler_params=_cparams(("parallel",)),
    )(proj3, proj3, proj3, conv_prev, ssm_prev, conv_w, conv_b, dt_bias, a_log, d_skip, norm_w)


def _rope_tables(pos):
    half = HEAD_DIM // 2
    inv_freq = ROPE_THETA ** (-jnp.arange(half, dtype=F32) / half)
    ang = pos.astype(F32)[:, None] * inv_freq[None, :]
    cos, sin = jnp.cos(ang), jnp.sin(ang)
    reps = LANES // HEAD_DIM
    cos_t = jnp.tile(jnp.concatenate([cos, cos], axis=1), (1, reps))
    sin_t = jnp.tile(jnp.concatenate([-sin, sin], axis=1), (1, reps))
    return cos_t, sin_t


def _rearrange_w_in(w):
    n_gate = 3 * D_MODEL
    wb = W_BRANCH
    ssm = w[:, 3 * wb:3 * wb + (C_DT - C_Z) + 8]
    d0 = 3 * wb + ssm.shape[1]
    pad = jnp.zeros((w.shape[0], C_DQ - C_Z - ssm.shape[1]), w.dtype)
    return jnp.concatenate([w[:, w.shape[1] - n_gate:], w[:, :wb], ssm, pad, w[:, d0:d0 + wb],
                            w[:, wb:3 * wb], w[:, d0 + wb:d0 + 3 * wb]], axis=1)


def _pad_lanes(v):
    v = v.reshape(1, -1)
    return jnp.pad(v, ((0, 0), (0, LANES - v.shape[1])))


def kernel(x_prompt, x_sample, cache_sb_k, cache_sb_v, cache_diff_k, cache_diff_v, state_ssm, state_conv,
           page_table, norm1_w, w_in, conv_w, conv_b, dt_bias, a_log, d_skip, ssm_norm_w,
           lam_q1, lam_k1, lam_q2, lam_k2, diff_norm_w, w_sb_out, w_ssm_out, w_diff_out, w_o,
           norm2_w, w_group, w_expert, w_gate, w_up, w_down, final_norm_w):
    bp, seq, d = x_prompt.shape
    bs = x_sample.shape[0]
    depth = w_in.shape[0]
    n_pages = page_table.shape[1]
    page = cache_sb_k.shape[2]
    sb_heads, diff_heads = cache_sb_k.shape[3], cache_diff_k.shape[3]
    n_heads = state_ssm.shape[2]
    cdim = state_conv.shape[3]
    n_groups = (cdim - n_heads * HEAD_DIM) // (2 * SSM_STATE)
    tp = bp * seq
    tq = min(256, seq)
    tq_diff = min(512, seq)
    w_gate_b, w_up_b, w_down_b = w_gate.astype(BF16), w_up.astype(BF16), w_down.astype(BF16)
    tm_proj = min(1024, seq)
    tm_merge = min(512, seq)
    tm_moe = min(1024, seq)
    pages_per_step = math.gcd(n_pages, 8)

    sb_k_t = jnp.transpose(cache_sb_k, (0, 1, 3, 4, 2))
    sb_v_t = jnp.transpose(cache_sb_v, (0, 1, 3, 4, 2))
    diff_k_r = cache_diff_k.reshape(depth, cache_diff_k.shape[1], page * diff_heads, 2 * HEAD_DIM)
    diff_v_r = cache_diff_v.reshape(depth, cache_diff_v.shape[1], page * diff_heads, 2 * HEAD_DIM)

    cos_p, sin_p = _rope_tables(jnp.arange(seq))
    cos_s, sin_s = _rope_tables(jnp.full((bs,), n_pages * page, jnp.int32))

    xp = x_prompt.reshape(tp, d)
    xs = x_sample.reshape(bs, d)
    outs = {k: [] for k in ("pssm", "pconv", "sk", "sv", "sdk", "sdv", "sssm", "sconv")}
    prompt_rows = None
    row = lambda v: v.reshape(1, -1)
    for l in range(depth):
        lam_init = 0.8 - 0.6 * math.exp(-0.3 * l)
        last = l == depth - 1
        w_in_f = _rearrange_w_in(w_in[l])
        lam_vecs = jnp.stack([lam_q1[l], lam_k1[l], lam_q2[l], lam_k2[l]])
        w_route = jnp.pad(jnp.concatenate([w_group[l], w_expert[l]], axis=1),
                          ((0, 0), (0, LANES - MOE_GROUPS - MOE_GROUPS * MOE_PER_GROUP)))
        ssd_w = (conv_w[l], row(conv_b[l]), _pad_lanes(dt_bias[l]), _pad_lanes(a_log[l]), _pad_lanes(d_skip[l]),
                 row(ssm_norm_w[l]))

        w_in_b = w_in_f.astype(BF16)
        proj, *prompt_rows = _inproj_prompt(xp, row(norm1_w[l]), w_in_b, w_in_b[:, C_SBK:C_DK].T, cos_p, sin_p,
                                            prompt_rows, l, depth=depth, batch=bp, seq=seq, tm=tm_proj)
        k_t, v_t, dk_rows, dv_rows = prompt_rows
        o_sb = _sb_prompt(proj, k_t, v_t, l, batch=bp, seq=seq, tq=tq)
        o_diff = _diff_prompt(proj, dk_rows, dv_rows, l, lam_vecs, row(diff_norm_w[l]), batch=bp, seq=seq,
                              tq=tq_diff, lam_init=lam_init)
        y_ssm, ssm_p, conv_p = _ssd_prompt(proj, *ssd_w, batch=bp, seq=seq, n_heads=n_heads, n_groups=n_groups)
        xp = _merge(xp, proj, o_sb, y_ssm, o_diff, w_sb_out[l].astype(BF16), w_ssm_out[l].astype(BF16),
                    w_diff_out[l].astype(BF16), w_o[l].astype(BF16), precise=False, tm=tm_merge)
        xp = _moe(xp, row(norm2_w[l]), w_route, w_gate_b, w_up_b, w_down_b, row(final_norm_w), l,
                  precise=False, final=last, tm=tm_moe)
        outs["pssm"].append(ssm_p)
        outs["pconv"].append(conv_p)

        proj_s = _inproj(xs, row(norm1_w[l]), w_in_f, cos_s, sin_s)
        proj3 = proj_s.reshape(bs, 1, PROJ_W)
        sec = lambda c0: proj3[:, :, c0:c0 + W_BRANCH]
        o_sb_s = _sb_sample(page_table, sec(C_SBQ), sb_k_t, sb_v_t, l, pages_per_step=pages_per_step)
        o_diff_s = _diff_sample(page_table, sec(C_DQ), sec(C_DK), sec(C_DV), lam_vecs, row(diff_norm_w[l]),
                                diff_k_r, diff_v_r, l, pages_per_step=math.gcd(n_pages, DIFF_PAGES_PER_STEP),
                                n_heads=diff_heads,
                                lam_init=lam_init)
        y_ssm_s, ssm_s, conv_s = _ssd_sample(proj3, state_conv[l], state_ssm[l], *ssd_w,
                                             n_heads=n_heads, n_groups=n_groups)
        xs = _merge(xs, proj_s, o_sb_s.reshape(bs, -1), y_ssm_s.reshape(bs, -1), o_diff_s.reshape(bs, -1),
                    w_sb_out[l], w_ssm_out[l], w_diff_out[l], w_o[l], precise=True, tm=bs)
        xs = _moe(xs, row(norm2_w[l]), w_route, w_gate, w_up, w_down, row(final_norm_w), l,
                  precise=True, final=last, tm=bs)
        outs["sk"].append(proj_s[:, C_SBK:C_SBK + W_BRANCH].reshape(bs, 1, sb_heads, HEAD_DIM))
        outs["sv"].append(proj_s[:, C_SBV:C_SBV + W_BRANCH].reshape(bs, 1, sb_heads, HEAD_DIM))
        outs["sdk"].append(proj_s[:, C_DK:C_DK + W_BRANCH].reshape(bs, 1, diff_heads, 2 * HEAD_DIM))
        outs["sdv"].append(proj_s[:, C_DV:C_DV + W_BRANCH].reshape(bs, 1, diff_heads, 2 * HEAD_DIM))
        outs["sssm"].append(ssm_s)
        outs["sconv"].append(conv_s)

    st = {k: jnp.stack(v) for k, v in outs.items()}
    k_t, v_t, dk_rows, dv_rows = prompt_rows
    heads_last = lambda a: jnp.transpose(a.reshape(depth, bp, sb_heads, HEAD_DIM, seq), (0, 1, 4, 2, 3))
    return (xp.reshape(bp, seq, d), xs.reshape(bs, 1, d),
            heads_last(k_t), heads_last(v_t),
            dk_rows.reshape(depth, bp, seq, diff_heads, 2 * HEAD_DIM),
            dv_rows.reshape(depth, bp, seq, diff_heads, 2 * HEAD_DIM), st["pssm"], st["pconv"],
            st["sk"], st["sv"], st["sdk"], st["sdv"], st["sssm"], st["sconv"])
```
